```python
import jax
import jax.numpy as jnp
from jax import lax
import numpy as np


D_MODEL = 2048
BATCH = 4
SEQ = 8192
DEPTH = 4

GRID_W = 64
CTX_LEN = 256
N_EVEN = (DEPTH + 1) // 2
N_ODD = DEPTH // 2
N_MOD = 6
EPS = 1e-6

GLA_HEADS = 4
GLA_DV = D_MODEL // 2
GLA_DK = GLA_DV // 2
GLA_HK = GLA_DK // GLA_HEADS
GLA_HV = GLA_DV // GLA_HEADS
GLA_RANK = 16
GLA_GATE_NORM = 16.0
GLA_CHUNK = 64
ROPE_BASE = 10000.0

FNET_DIM = D_MODEL - GLA_DV
FNET_GROUPS = 4
FNET_GROUP_DIM = FNET_DIM // FNET_GROUPS

AB_SPLIT = (GLA_DK, GLA_DK, GLA_DV, GLA_DV, GLA_RANK, GLA_RANK, FNET_DIM)
N_IN_AB = sum(AB_SPLIT)
MIX_AB = GLA_DV + FNET_DIM

NA_HEADS = 16
NA_HD = D_MODEL // NA_HEADS
NA_KR_MAX = 8
NA_KC = 16

N_EXPERTS = 16
EC_CAPACITY_FACTOR = 2
D_FF_EXPERT = D_MODEL // 2

kernel_name = 'hybrid_gla_fnet_natten_ecmoe_dit'


def rmsnorm(x, g):
    xf = x.astype(jnp.float32)
    y = xf * lax.rsqrt(jnp.mean(xf * xf, axis=-1, keepdims=True) + EPS)
    return y.astype(x.dtype) * g


def modulate(h, shift, scale):
    return h * (1 + scale) + shift


def ada_params(cond, w, b):
    m = jax.nn.silu(cond) @ w + b
    return jnp.split(m, N_MOD, axis=-1)


def to_heads(a, d):
    b, t = a.shape[:2]
    return a.reshape(b, t, -1, d).transpose(0, 2, 1, 3)


def axial_rope_angles(n_tok, head_dim):
    n_freq = head_dim // 4
    inv = ROPE_BASE ** (-jnp.arange(n_freq, dtype=jnp.float32) / n_freq)
    t = jnp.arange(n_tok)
    row = (t // GRID_W).astype(jnp.float32)
    col = (t % GRID_W).astype(jnp.float32)
    return row[:, None] * inv[None], col[:, None] * inv[None]


def rotate(x, ang):
    m = ang.shape[-1]
    cos = jnp.cos(ang).astype(x.dtype)
    sin = jnp.sin(ang).astype(x.dtype)
    x1, x2 = x[..., :m], x[..., m:]
    return jnp.concatenate([x1 * cos - x2 * sin, x1 * sin + x2 * cos], axis=-1)


def apply_axial_rope(x, ang_row, ang_col):
    half = x.shape[-1] // 2
    return jnp.concatenate([rotate(x[..., :half], ang_row), rotate(x[..., half:], ang_col)], axis=-1)


def gla_chunked(q, k, v, log_a, s0):
    B, H, T, dk = q.shape
    dv = v.shape[-1]
    n, C = T // GLA_CHUNK, GLA_CHUNK
    f32 = jnp.float32
    q = q.astype(f32).reshape(B, H, n, C, dk)
    k = k.astype(f32).reshape(B, H, n, C, dk)
    v = v.astype(f32).reshape(B, H, n, C, dv)
    b = jnp.cumsum(log_a.astype(f32).reshape(B, H, n, C, dk), axis=3)
    b_last = b[:, :, :, -1:, :]
    qe = q * jnp.exp(b)
    ke = k * jnp.exp(-b)
    kd = k * jnp.exp(b_last - b)
    mask = jnp.tril(jnp.ones((C, C), dtype=bool))
    att = jnp.where(mask, jnp.einsum('bhnid,bhnjd->bhnij', qe, ke), 0.0)
    o_intra = jnp.einsum('bhnij,bhnje->bhnie', att, v)
    u = jnp.einsum('bhnjd,bhnje->bhnde', kd, v)
    decay = jnp.exp(b_last[:, :, :, 0, :])

    def step(s, xs):
        dec, uc = xs
        return dec[..., None] * s + uc, s

    s_fin, s_in = lax.scan(step, s0.astype(f32), (jnp.moveaxis(decay, 2, 0), jnp.moveaxis(u, 2, 0)))
    s_in = jnp.moveaxis(s_in, 0, 2)
    o = o_intra + jnp.einsum('bhnid,bhnde->bhnie', qe, s_in)
    return o.reshape(B, H, T, dv), s_fin


def gla_direction(qc, kc, vc, lac, ql, kl, vl, lal, reverse):
    if reverse:
        qc, kc, vc, lac, ql, kl, vl, lal = [jnp.flip(a, axis=2) for a in (qc, kc, vc, lac, ql, kl, vl, lal)]
    B, H, _, dk = qc.shape
    s0 = jnp.zeros((B, H, dk, vc.shape[-1]), jnp.float32)
    o_c, s_c = gla_chunked(qc, kc, vc, lac, s0)
    o_l, _ = gla_chunked(ql, kl, vl, lal, s_c)
    if reverse:
        o_c, o_l = jnp.flip(o_c, axis=2), jnp.flip(o_l, axis=2)
    return o_c, o_l


def gla_inputs(p, a_up_f, a_b_f, a_up_b, a_b_b, rope):
    offs = list(np.cumsum(AB_SPLIT)[:-1])
    q, k, v, g, lr_f, lr_b, u = jnp.split(p, offs, axis=-1)
    q = to_heads(q, GLA_HK) * (GLA_HK ** -0.5)
    k = to_heads(k, GLA_HK)
    v = to_heads(v, GLA_HV)
    la_f = to_heads(jax.nn.log_sigmoid((lr_f @ a_up_f + a_b_f).astype(jnp.float32)) / GLA_GATE_NORM, GLA_HK)
    la_b = to_heads(jax.nn.log_sigmoid((lr_b @ a_up_b + a_b_b).astype(jnp.float32)) / GLA_GATE_NORM, GLA_HK)
    if rope is not None:
        q = apply_axial_rope(q, *rope)
        k = apply_axial_rope(k, *rope)
    return q, k, v, la_f, la_b, g, u


def gla_output(o, g, norm_g):
    B, H, T, dv = o.shape
    o = o * lax.rsqrt(jnp.mean(o * o, axis=-1, keepdims=True) + EPS)
    o = o * norm_g.reshape(H, 1, dv).astype(jnp.float32)
    o = o.transpose(0, 2, 1, 3).reshape(B, T, H * dv).astype(g.dtype)
    return o * jax.nn.silu(g)


def fourier_mix(u):
    B, T, _ = u.shape
    ug = u.reshape(B, T, FNET_GROUPS, FNET_GROUP_DIM).astype(jnp.float32)
    y = jnp.real(jnp.fft.fft2(ug, axes=(1, 3), norm='ortho'))
    return y.reshape(B, T, FNET_DIM).astype(u.dtype)


def mixer_gla_fnet(h_ctx, h_lat, w_in, w_out, a_up_f, a_b_f, a_up_b, a_b_b, norm_g, last):
    T = h_lat.shape[1]
    rope = axial_rope_angles(T, GLA_HK)
    qc, kc, vc, lafc, labc, gc, uc = gla_inputs(h_ctx @ w_in, a_up_f, a_b_f, a_up_b, a_b_b, None)
    ql, kl, vl, lafl, labl, gl, ul = gla_inputs(h_lat @ w_in, a_up_f, a_b_f, a_up_b, a_b_b, rope)
    ofc, ofl = gla_direction(qc, kc, vc, lafc, ql, kl, vl, lafl, False)
    obc, obl = gla_direction(qc, kc, vc, labc, ql, kl, vl, labl, True)
    y_lat = jnp.concatenate([gla_output(ofl + obl, gl, norm_g), fourier_mix(ul)], axis=-1) @ w_out
    if last:
        return None, y_lat
    y_ctx = jnp.concatenate([gla_output(ofc + obc, gc, norm_g), fourier_mix(uc)], axis=-1) @ w_out
    return y_ctx, y_lat


def mixer_neighbourhood(h_ctx, h_lat, w_qkv, w_out, rpb, last):
    B, T, D = h_lat.shape
    rows = T // GRID_W
    kr = min(NA_KR_MAX, rows)
    scale = NA_HD ** -0.5
    q_l, k_l, v_l = jnp.split(h_lat @ w_qkv, 3, axis=-1)
    q_c, k_c, v_c = jnp.split(h_ctx @ w_qkv, 3, axis=-1)
    L = h_ctx.shape[1]
    heads = lambda a: a.reshape(a.shape[0], a.shape[1], NA_HEADS, NA_HD)
    q_c, k_c, v_c = heads(q_c) * scale, heads(k_c), heads(v_c)
    grid = lambda a: a.reshape(B, rows, GRID_W, NA_HEADS, NA_HD)
    q_g, k_g, v_g = grid(q_l * scale), grid(k_l), grid(v_l)
    row_start = jnp.clip(jnp.arange(rows) - kr // 2, 0, rows - kr)
    col_start = jnp.clip(jnp.arange(GRID_W) - NA_KC // 2, 0, GRID_W - NA_KC)
    col_idx = col_start[:, None] + jnp.arange(NA_KC)[None, :]
    rel_col = col_idx - jnp.arange(GRID_W)[:, None] + (NA_KC - 1)
    rpb_col = rpb[:, :, rel_col]
    n_win = kr * NA_KC

    def row_block(r):
        rs = row_start[r]
        k_rows = lax.dynamic_slice_in_dim(k_g, rs, kr, axis=1)
        v_rows = lax.dynamic_slice_in_dim(v_g, rs, kr, axis=1)
        k_win = k_rows[:, :, col_idx]
        v_win = v_rows[:, :, col_idx]
        q_r = lax.dynamic_index_in_dim(q_g, r, axis=1, keepdims=False)
        rel_row = rs + jnp.arange(kr) - r + (NA_KR_MAX - 1)
        bias = jnp.transpose(rpb_col[:, rel_row], (0, 2, 1, 3))
        s_win = jnp.einsum('bwhd,brwkhd->bhwrk', q_r, k_win) + bias[None]
        s_ctx = jnp.einsum('bwhd,bchd->bhwc', q_r, k_c)
        logits = jnp.concatenate([s_win.reshape(B, NA_HEADS, GRID_W, n_win), s_ctx], axis=-1)
        p = jax.nn.softmax(logits.astype(jnp.float32), axis=-1).astype(v_g.dtype)
        p_win = p[..., :n_win].reshape(B, NA_HEADS, GRID_W, kr, NA_KC)
        p_ctx = p[..., n_win:]
        return (jnp.einsum('bhwrk,brwkhd->bwhd', p_win, v_win)
                + jnp.einsum('bhwc,bchd->bwhd', p_ctx, v_c))

    o = lax.map(row_block, jnp.arange(rows))
    y_lat = jnp.moveaxis(o, 0, 1).reshape(B, T, D) @ w_out
    if last:
        return None, y_lat
    s = jnp.einsum('bqhd,bkhd->bhqk', q_c, k_c)
    p = jax.nn.softmax(s.astype(jnp.float32), axis=-1).astype(v_c.dtype)
    y_ctx = jnp.einsum('bhqk,bkhd->bqhd', p, v_c).reshape(B, L, D) @ w_out
    return y_ctx, y_lat


def expert_choice_moe(h, router_w, w1, w3, w2):
    B, T, D = h.shape
    cap = EC_CAPACITY_FACTOR * T // N_EXPERTS
    aff = jax.nn.softmax((h @ router_w).astype(jnp.float32), axis=-1)
    gate, idx = lax.top_k(jnp.swapaxes(aff, 1, 2), cap)
    xe = jax.vmap(lambda hb, ib: hb[ib])(h, idx)
    a = jnp.einsum('becd,edf->becf', xe, w1)
    b = jnp.einsum('becd,edf->becf', xe, w3)
    ye = jnp.einsum('becf,efd->becd', jax.nn.silu(a) * b, w2) * gate[..., None].astype(h.dtype)
    return jax.vmap(lambda ib, yb: jnp.zeros((T, D), yb.dtype).at[ib.reshape(-1)].add(yb.reshape(-1, D)))(idx, ye)


def setup_inputs(seed: int = 0) -> dict:
    key = jax.random.key(seed)
    ks = jax.random.split(key, 24)
    f32 = jnp.float32
    D = D_MODEL

    def nrm(k, shape, s):
        return jax.random.normal(k, shape, f32) * s

    return {
        'x': nrm(ks[0], (BATCH, SEQ, D), 1.0),
        'c': nrm(ks[1], (BATCH, D), 1.0),
        'ctx': nrm(ks[2], (BATCH, CTX_LEN, D), 1.0),
        'c_ctx': nrm(ks[3], (D,), 1.0),
        'ada_w': nrm(ks[4], (DEPTH, D, N_MOD * D), 0.5 * D ** -0.5),
        'ada_b': nrm(ks[5], (DEPTH, N_MOD * D), 0.01),
        'norm1_g': 1.0 + nrm(ks[6], (DEPTH, D), 0.1),
        'norm2_g': 1.0 + nrm(ks[7], (DEPTH, D), 0.1),
        'ab_w_in': nrm(ks[8], (N_EVEN, D, N_IN_AB), D ** -0.5),
        'ab_w_out': nrm(ks[9], (N_EVEN, MIX_AB, D), MIX_AB ** -0.5),
        'gla_a_up_f': nrm(ks[10], (N_EVEN, GLA_RANK, GLA_DK), GLA_RANK ** -0.5),
        'gla_a_b_f': nrm(ks[11], (N_EVEN, GLA_DK), 0.01),
        'gla_a_up_b': nrm(ks[12], (N_EVEN, GLA_RANK, GLA_DK), GLA_RANK ** -0.5),
        'gla_a_b_b': nrm(ks[13], (N_EVEN, GLA_DK), 0.01),
        'gla_norm_g': 1.0 + nrm(ks[14], (N_EVEN, GLA_DV), 0.1),
        'na_w_qkv': nrm(ks[15], (N_ODD, D, 3 * D), D ** -0.5),
        'na_w_out': nrm(ks[16], (N_ODD, D, D), D ** -0.5),
        'na_rpb': nrm(ks[17], (N_ODD, NA_HEADS, 2 * NA_KR_MAX - 1, 2 * NA_KC - 1), 0.1),
        'router_w': nrm(ks[18], (DEPTH, D, N_EXPERTS), D ** -0.5),
        'exp_w1': nrm(ks[19], (DEPTH, N_EXPERTS, D, D_FF_EXPERT), D ** -0.5),
        'exp_w3': nrm(ks[20], (DEPTH, N_EXPERTS, D, D_FF_EXPERT), D ** -0.5),
        'exp_w2': nrm(ks[21], (DEPTH, N_EXPERTS, D_FF_EXPERT, D), D_FF_EXPERT ** -0.5),
        'final_g': 1.0 + nrm(ks[22], (D,), 0.1),
    }


def reference(x, c, ctx, c_ctx, ada_w, ada_b, norm1_g, norm2_g, ab_w_in, ab_w_out,
              gla_a_up_f, gla_a_b_f, gla_a_up_b, gla_a_b_b, gla_norm_g,
              na_w_qkv, na_w_out, na_rpb, router_w, exp_w1, exp_w3, exp_w2, final_g):
    xl, xc = x, ctx
    for l in range(DEPTH):
        last = l == DEPTH - 1
        sh1, sc1, g1, sh2, sc2, g2 = [m[:, None, :] for m in ada_params(c, ada_w[l], ada_b[l])]
        csh1, csc1, cg1, csh2, csc2, cg2 = ada_params(c_ctx, ada_w[l], ada_b[l])
        hl = modulate(rmsnorm(xl, norm1_g[l]), sh1, sc1)
        hc = modulate(rmsnorm(xc, norm1_g[l]), csh1, csc1)
        if l % 2 == 0:
            e = l // 2
            yc, yl = mixer_gla_fnet(hc, hl, ab_w_in[e], ab_w_out[e], gla_a_up_f[e], gla_a_b_f[e],
                                    gla_a_up_b[e], gla_a_b_b[e], gla_norm_g[e], last)
        else:
            o = l // 2
            yc, yl = mixer_neighbourhood(hc, hl, na_w_qkv[o], na_w_out[o], na_rpb[o], last)
        xl = xl + g1 * yl
        hl = modulate(rmsnorm(xl, norm2_g[l]), sh2, sc2)
        xl = xl + g2 * expert_choice_moe(hl, router_w[l], exp_w1[l], exp_w3[l], exp_w2[l])
        if not last:
            xc = xc + cg1 * yc
            hc = modulate(rmsnorm(xc, norm2_g[l]), csh2, csc2)
            xc = xc + cg2 * expert_choice_moe(hc, router_w[l], exp_w1[l], exp_w3[l], exp_w2[l])
    return rmsnorm(xl, final_g)
```

```python
import functools

import jax
import jax.numpy as jnp
import numpy as np
from jax import lax
from jax.experimental import pallas as pl
from jax.experimental.pallas import tpu as pltpu

F32 = jnp.float32
BF16 = jnp.bfloat16

GRID_W = 64
N_MOD = 6
EPS = 1e-6
GLA_HEADS = 4
GLA_RANK = 16
GLA_GATE_NORM = 16.0
GLA_CHUNK = 64
ROPE_BASE = 10000.0
FNET_GROUPS = 4
NA_HEADS = 16
NA_KR_MAX = 8
NA_KC = 16
N_EXPERTS = 16
EC_CAPACITY_FACTOR = 2

LANES = 128
VMEM_LIMIT_BYTES = 56 * 1024 * 1024
MASK_VALUE = -1e30

NA_ROWS_PER_BLOCK = 4
NA_HEADS_PER_STEP = 4


def _cparams(*sem):
    return pltpu.CompilerParams(dimension_semantics=sem, vmem_limit_bytes=VMEM_LIMIT_BYTES)


def _row_tile(t, cap):
    if t <= cap:
        return t
    for step in (LANES, 8):
        for tm in range(cap - cap % step, 0, -step):
            if t % tm == 0:
                return tm
    raise ValueError(f"no tile for {t} under {cap}")


def _ada_body(c_ref, w_ref, b_ref, o_ref):
    c = c_ref[...]
    a = (c * jax.nn.sigmoid(c)).astype(BF16)
    acc = jnp.dot(a, w_ref[0].astype(BF16), preferred_element_type=F32)
    o_ref[0] = acc + b_ref[0]


def ada_all_layers(cond, ada_w, ada_b):
    depth, d, n = ada_w.shape
    r = cond.shape[0]
    tn = _row_tile(n, 1024)
    return pl.pallas_call(
        _ada_body,
        grid=(depth, n // tn),
        in_specs=[
            pl.BlockSpec((r, d), lambda l, j: (0, 0)),
            pl.BlockSpec((1, d, tn), lambda l, j: (l, 0, j)),
            pl.BlockSpec((1, 1, tn), lambda l, j: (l, 0, j)),
        ],
        out_specs=pl.BlockSpec((1, r, tn), lambda l, j: (l, 0, j)),
        out_shape=jax.ShapeDtypeStruct((depth, r, n), F32),
        compiler_params=_cparams("parallel", "parallel"),
        name="ada_params",
    )(cond, ada_w, ada_b.reshape(depth, 1, n))


def _norm_mod(x, g, shift, scale):
    ms = jnp.mean(x * x, axis=-1, keepdims=True)
    y = x * lax.rsqrt(ms + EPS)
    return (y * g) * (1.0 + scale) + shift


def _nm_mm_body(x_ref, g_ref, sh_ref, sc_ref, w_ref, cs_ref, o_ref, h_scr):
    @pl.when(pl.program_id(2) == 0)
    def _():
        h = _norm_mod(x_ref[0], g_ref[...], sh_ref[0], sc_ref[0])
        h_scr[...] = h.astype(h_scr.dtype)

    acc = jnp.dot(h_scr[...], w_ref[...], preferred_element_type=F32)
    o_ref[0] = (acc * cs_ref[...]).astype(o_ref.dtype)


def _nm_mm_aux_body(x_ref, g_ref, sh_ref, sc_ref, w_ref, cs_ref, wa_ref, o_ref, oa_ref, h_scr):
    @pl.when(pl.program_id(2) == 0)
    def _():
        h = _norm_mod(x_ref[0], g_ref[...], sh_ref[0], sc_ref[0])
        h_scr[...] = h.astype(h_scr.dtype)
        oa_ref[0] = jnp.dot(h, wa_ref[...], preferred_element_type=F32,
                            precision=lax.Precision.HIGHEST)

    acc = jnp.dot(h_scr[...], w_ref[...], preferred_element_type=F32)
    o_ref[0] = (acc * cs_ref[...]).astype(o_ref.dtype)


def norm_mod_matmul(x, g, shift, scale, w, col_scale, *, out_dtype=BF16, w_aux=None, tm=512, tn=512):
    b, t, d = x.shape
    n = w.shape[1]
    tm = _row_tile(t, tm)
    tn = _row_tile(n, tn)
    in_specs = [
        pl.BlockSpec((1, tm, d), lambda bi, i, j: (bi, i, 0)),
        pl.BlockSpec((1, d), lambda bi, i, j: (0, 0)),
        pl.BlockSpec((1, 1, d), lambda bi, i, j: (bi, 0, 0)),
        pl.BlockSpec((1, 1, d), lambda bi, i, j: (bi, 0, 0)),
        pl.BlockSpec((d, tn), lambda bi, i, j: (0, j)),
        pl.BlockSpec((1, tn), lambda bi, i, j: (0, j)),
    ]
    args = [x, g.reshape(1, d), shift, scale, w, col_scale.reshape(1, n)]
    out_specs = pl.BlockSpec((1, tm, tn), lambda bi, i, j: (bi, i, j))
    out_shape = jax.ShapeDtypeStruct((b, t, n), out_dtype)
    body = _nm_mm_body
    if w_aux is not None:
        na = w_aux.shape[1]
        in_specs.append(pl.BlockSpec((d, na), lambda bi, i, j: (0, 0)))
        args.append(w_aux)
        out_specs = (out_specs, pl.BlockSpec((1, tm, na), lambda bi, i, j: (bi, i, 0)))
        out_shape = (out_shape, jax.ShapeDtypeStruct((b, t, na), F32))
        body = _nm_mm_aux_body
    return pl.pallas_call(
        body,
        grid=(b, t // tm, n // tn),
        in_specs=in_specs,
        out_specs=out_specs,
        out_shape=out_shape,
        scratch_shapes=[pltpu.VMEM((tm, d), BF16)],
        compiler_params=_cparams("parallel", "parallel", "arbitrary"),
        name="norm_mod_matmul",
    )(*args)


def _nm_router_body(x_ref, g_ref, sh_ref, sc_ref, rw_ref, h_ref, lg_ref):
    h = _norm_mod(x_ref[0], g_ref[...], sh_ref[0], sc_ref[0])
    h_ref[0] = h.astype(h_ref.dtype)
    lg_ref[0] = jnp.dot(h, rw_ref[...], preferred_element_type=F32, precision=lax.Precision.HIGHEST)


def norm_mod_router(x, g, shift, scale, router_w, *, tm=512):
    b, t, d = x.shape
    e = router_w.shape[1]
    rw = jnp.pad(router_w, ((0, 0), (0, LANES - e)))
    tm = _row_tile(t, tm)
    h, lg = pl.pallas_call(
        _nm_router_body,
        grid=(b, t // tm),
        in_specs=[
            pl.BlockSpec((1, tm, d), lambda bi, i: (bi, i, 0)),
            pl.BlockSpec((1, d), lambda bi, i: (0, 0)),
            pl.BlockSpec((1, 1, d), lambda bi, i: (bi, 0, 0)),
            pl.BlockSpec((1, 1, d), lambda bi, i: (bi, 0, 0)),
            pl.BlockSpec((d, LANES), lambda bi, i: (0, 0)),
        ],
        out_specs=(
            pl.BlockSpec((1, tm, d), lambda bi, i: (bi, i, 0)),
            pl.BlockSpec((1, tm, LANES), lambda bi, i: (bi, i, 0)),
        ),
        out_shape=(jax.ShapeDtypeStruct((b, t, d), BF16), jax.ShapeDtypeStruct((b, t, LANES), F32)),
        compiler_params=_cparams("parallel", "parallel"),
        name="norm_mod_router",
    )(x, g.reshape(1, d), shift, scale, rw)
    return h, lg[..., :e]


def _mm_res_body(a_ref, w_ref, r_ref, g_ref, o_ref):
    acc = jnp.dot(a_ref[0], w_ref[...], preferred_element_type=F32)
    o_ref[0] = r_ref[0] + g_ref[0] * acc


def matmul_gated_residual(a, w, res, gate, *, tm=512, tn=512):
    b, t, k = a.shape
    n = w.shape[1]
    tm = _row_tile(t, tm)
    tn = _row_tile(n, tn)
    return pl.pallas_call(
        _mm_res_body,
        grid=(b, t // tm, n // tn),
        in_specs=[
            pl.BlockSpec((1, tm, k), lambda bi, i, j: (bi, i, 0)),
            pl.BlockSpec((k, tn), lambda bi, i, j: (0, j)),
            pl.BlockSpec((1, tm, tn), lambda bi, i, j: (bi, i, j)),
            pl.BlockSpec((1, 1, tn), lambda bi, i, j: (bi, 0, j)),
        ],
        out_specs=pl.BlockSpec((1, tm, tn), lambda bi, i, j: (bi, i, j)),
        out_shape=jax.ShapeDtypeStruct((b, t, n), F32),
        compiler_params=_cparams("parallel", "parallel", "parallel"),
        name="matmul_gated_residual",
    )(a, w, res, gate)


def _rmsnorm_body(x_ref, g_ref, o_ref):
    x = x_ref[0]
    ms = jnp.mean(x * x, axis=-1, keepdims=True)
    o_ref[0] = (x * lax.rsqrt(ms + EPS)) * g_ref[...]


def rmsnorm_final(x, g, *, tm=512):
    b, t, d = x.shape
    tm = _row_tile(t, tm)
    return pl.pallas_call(
        _rmsnorm_body,
        grid=(b, t // tm),
        in_specs=[pl.BlockSpec((1, tm, d), lambda bi, i: (bi, i, 0)),
                  pl.BlockSpec((1, d), lambda bi, i: (0, 0))],
        out_specs=pl.BlockSpec((1, tm, d), lambda bi, i: (bi, i, 0)),
        out_shape=jax.ShapeDtypeStruct((b, t, d), F32),
        compiler_params=_cparams("parallel", "parallel"),
        name="rmsnorm_final",
    )(x, g.reshape(1, d))


def _na_body(q_ref, k0_ref, k1_ref, k2_ref, v0_ref, v1_ref, v2_ref, kc_ref, vc_ref, bias_ref, o_ref,
             *, heads, hd, qb):
    dn = (((1,), (1,)), ((), ()))
    k_refs = (k0_ref, k1_ref, k2_ref)
    v_refs = (v0_ref, v1_ref, v2_ref)
    for g in range(heads):
        sl = slice(g * hd, (g + 1) * hd)
        q = q_ref[0, :, sl]
        s = [lax.dot_general(q, k_refs[i][0, :, sl], dn, preferred_element_type=F32)
             + bias_ref[0, g, :, i * qb:(i + 1) * qb] for i in range(3)]
        s.append(lax.dot_general(q, kc_ref[0, :, sl], dn, preferred_element_type=F32))
        m = functools.reduce(jnp.maximum, [jnp.max(si, axis=-1, keepdims=True) for si in s])
        p = [jnp.exp(si - m) for si in s]
        l = functools.reduce(jnp.add, [jnp.sum(pi, axis=-1, keepdims=True) for pi in p])
        vals = [v_refs[i][0, :, sl] for i in range(3)] + [vc_ref[0, :, sl]]
        o = functools.reduce(jnp.add, [jnp.dot(pi.astype(BF16), vi, preferred_element_type=F32)
                                       for pi, vi in zip(p, vals)])
        o_ref[0, :, sl] = (o / l).astype(o_ref.dtype)


def _na_bias_table(rpb, rows):
    r_blk = NA_ROWS_PER_BLOCK
    nblk = rows // r_blk
    kr = min(NA_KR_MAX, rows)
    tables = []
    for rb in (0, 1, nblk - 1):
        ks = min(max(rb - 1, 0), nblk - 3) * r_blk
        q_row = rb * r_blk + np.arange(r_blk)
        rs = np.clip(q_row - kr // 2, 0, rows - kr)
        k_row = ks + np.arange(3 * r_blk)
        ok_row = (k_row[None, :] >= rs[:, None]) & (k_row[None, :] < rs[:, None] + kr)
        rel_row = np.clip(k_row[None, :] - q_row[:, None] + (NA_KR_MAX - 1), 0, 2 * NA_KR_MAX - 2)
        q_col = np.arange(GRID_W)
        cs = np.clip(q_col - NA_KC // 2, 0, GRID_W - NA_KC)
        k_col = np.arange(GRID_W)
        ok_col = (k_col[None, :] >= cs[:, None]) & (k_col[None, :] < cs[:, None] + NA_KC)
        rel_col = np.clip(k_col[None, :] - q_col[:, None] + (NA_KC - 1), 0, 2 * NA_KC - 2)
        t = rpb[:, rel_row][:, :, :, rel_col]
        ok = ok_row[:, :, None, None] & ok_col[None, None, :, :]
        t = jnp.where(ok[None], t, MASK_VALUE)
        t = jnp.transpose(t, (0, 1, 3, 2, 4))
        tables.append(t.reshape(rpb.shape[0], r_blk * GRID_W, 3 * r_blk * GRID_W))
    return jnp.stack(tables)


def neighbourhood_attention(qkv, qkv_ctx, rpb):
    b, t, d3 = qkv.shape
    d = d3 // 3
    l = qkv_ctx.shape[1]
    hd = d // NA_HEADS
    rows = t // GRID_W
    r_blk = NA_ROWS_PER_BLOCK
    qb = r_blk * GRID_W
    nblk = rows // r_blk
    assert rows % r_blk == 0 and nblk >= 3 and min(NA_KR_MAX, rows) + r_blk <= 3 * r_blk
    g = NA_HEADS_PER_STEP
    gw = g * hd
    nhg = NA_HEADS // g
    bias = _na_bias_table(rpb.astype(F32), rows)

    def kstart(rb):
        return jnp.clip(rb - 1, 0, nblk - 3)

    def pattern(rb):
        return jnp.where(rb == 0, 0, jnp.where(rb == nblk - 1, 2, 1))

    q_spec = pl.BlockSpec((1, qb, gw), lambda hg, bi, rb: (bi, rb, hg))
    k_specs = [pl.BlockSpec((1, qb, gw), functools.partial(
        lambda hg, bi, rb, i: (bi, kstart(rb) + i, nhg + hg), i=i)) for i in range(3)]
    v_specs = [pl.BlockSpec((1, qb, gw), functools.partial(
        lambda hg, bi, rb, i: (bi, kstart(rb) + i, 2 * nhg + hg), i=i)) for i in range(3)]
    kc_spec = pl.BlockSpec((1, l, gw), lambda hg, bi, rb: (bi, 0, nhg + hg))
    vc_spec = pl.BlockSpec((1, l, gw), lambda hg, bi, rb: (bi, 0, 2 * nhg + hg))
    bias_spec = pl.BlockSpec((1, g, qb, 3 * qb), lambda hg, bi, rb: (pattern(rb), hg, 0, 0))
    return pl.pallas_call(
        functools.partial(_na_body, heads=g, hd=hd, qb=qb),
        grid=(nhg, b, nblk),
        in_specs=[q_spec] + k_specs + v_specs + [kc_spec, vc_spec, bias_spec],
        out_specs=pl.BlockSpec((1, qb, gw), lambda hg, bi, rb: (bi, rb, hg)),
        out_shape=jax.ShapeDtypeStruct((b, t, d), BF16),
        compiler_params=_cparams("parallel", "parallel", "arbitrary"),
        name="neighbourhood_attention",
    )(qkv, qkv, qkv, qkv, qkv, qkv, qkv, qkv_ctx, qkv_ctx, bias)


def _moe_ffn_body(x_ref, w1_ref, w3_ref, w2_ref, gt_ref, o_ref):
    x = x_ref[0]
    a = jnp.dot(x, w1_ref[0], preferred_element_type=F32)
    bb = jnp.dot(x, w3_ref[0], preferred_element_type=F32)
    hm = (a * jax.nn.sigmoid(a) * bb).astype(BF16)
    y = jnp.dot(hm, w2_ref[0], preferred_element_type=F32)
    o_ref[0] = y * gt_ref[0]


def moe_ffn(xe, gate, w1, w3, w2, *, tm=256):
    e, m, d = xe.shape
    f = w1.shape[2]
    tm = _row_tile(m, tm)
    return pl.pallas_call(
        _moe_ffn_body,
        grid=(e, m // tm),
        in_specs=[
            pl.BlockSpec((1, tm, d), lambda ei, i: (ei, i, 0)),
            pl.BlockSpec((1, d, f), lambda ei, i: (ei, 0, 0)),
            pl.BlockSpec((1, d, f), lambda ei, i: (ei, 0, 0)),
            pl.BlockSpec((1, f, d), lambda ei, i: (ei, 0, 0)),
            pl.BlockSpec((1, tm, 1), lambda ei, i: (ei, i, 0)),
        ],
        out_specs=pl.BlockSpec((1, tm, d), lambda ei, i: (ei, i, 0)),
        out_shape=jax.ShapeDtypeStruct((e, m, d), F32),
        compiler_params=_cparams("parallel", "arbitrary"),
        name="moe_ffn",
    )(xe, w1, w3, w2, gate)


def expert_choice_moe(x, g, shift, scale, router_w, w1, w3, w2):
    b, t, d = x.shape
    e = router_w.shape[1]
    cap = EC_CAPACITY_FACTOR * t // e
    h, logits = norm_mod_router(x, g, shift, scale, router_w)
    aff = jax.nn.softmax(logits, axis=-1)
    gate, idx = lax.top_k(jnp.swapaxes(aff, 1, 2), cap)
    xe = jax.vmap(lambda hb, ib: hb[ib])(h, idx)
    xe = jnp.swapaxes(xe, 0, 1).reshape(e, b * cap, d)
    gt = jnp.swapaxes(gate, 0, 1).reshape(e, b * cap, 1)
    ye = moe_ffn(xe, gt, w1, w3, w2)
    ye = jnp.swapaxes(ye.reshape(e, b, cap, d), 0, 1)
    return jax.vmap(lambda ib, yb: jnp.zeros((t, d), yb.dtype).at[ib.reshape(-1)].add(
        yb.reshape(-1, d)))(idx, ye)


def _to_heads(a, dh):
    b, t = a.shape[:2]
    return a.reshape(b, t, -1, dh).transpose(0, 2, 1, 3)


def _axial_rope_angles(n_tok, head_dim):
    n_freq = head_dim // 4
    inv = ROPE_BASE ** (-jnp.arange(n_freq, dtype=F32) / n_freq)
    tt = jnp.arange(n_tok)
    row = (tt // GRID_W).astype(F32)
    col = (tt % GRID_W).astype(F32)
    return row[:, None] * inv[None], col[:, None] * inv[None]


def _rotate(x, ang):
    m = ang.shape[-1]
    cos = jnp.cos(ang).astype(x.dtype)
    sin = jnp.sin(ang).astype(x.dtype)
    x1, x2 = x[..., :m], x[..., m:]
    return jnp.concatenate([x1 * cos - x2 * sin, x1 * sin + x2 * cos], axis=-1)


def _apply_axial_rope(x, ang_row, ang_col):
    half = x.shape[-1] // 2
    return jnp.concatenate([_rotate(x[..., :half], ang_row), _rotate(x[..., half:], ang_col)], axis=-1)


def _gla_chunked(q, k, v, log_a, s0):
    bsz, hh, t, dk = q.shape
    dv = v.shape[-1]
    n, c = t // GLA_CHUNK, GLA_CHUNK
    q = q.astype(F32).reshape(bsz, hh, n, c, dk)
    k = k.astype(F32).reshape(bsz, hh, n, c, dk)
    v = v.astype(F32).reshape(bsz, hh, n, c, dv)
    bcs = jnp.cumsum(log_a.astype(F32).reshape(bsz, hh, n, c, dk), axis=3)
    b_last = bcs[:, :, :, -1:, :]
    qe = q * jnp.exp(bcs)
    ke = k * jnp.exp(-bcs)
    kd = k * jnp.exp(b_last - bcs)
    mask = jnp.tril(jnp.ones((c, c), dtype=bool))
    att = jnp.where(mask, jnp.einsum('bhnid,bhnjd->bhnij', qe, ke), 0.0)
    o_intra = jnp.einsum('bhnij,bhnje->bhnie', att, v)
    u = jnp.einsum('bhnjd,bhnje->bhnde', kd, v)
    decay = jnp.exp(b_last[:, :, :, 0, :])

    def step(s, xs):
        dec, uc = xs
        return dec[..., None] * s + uc, s

    s_fin, s_in = lax.scan(step, s0.astype(F32), (jnp.moveaxis(decay, 2, 0), jnp.moveaxis(u, 2, 0)))
    s_in = jnp.moveaxis(s_in, 0, 2)
    o = o_intra + jnp.einsum('bhnid,bhnde->bhnie', qe, s_in)
    return o.reshape(bsz, hh, t, dv), s_fin


def _gla_direction(qc, kc, vc, lac, ql, kl, vl, lal, reverse):
    if reverse:
        qc, kc, vc, lac, ql, kl, vl, lal = [jnp.flip(a, axis=2) for a in (qc, kc, vc, lac, ql, kl, vl, lal)]
    bsz, hh, _, dk = qc.shape
    s0 = jnp.zeros((bsz, hh, dk, vc.shape[-1]), F32)
    o_c, s_c = _gla_chunked(qc, kc, vc, lac, s0)
    o_l, _ = _gla_chunked(ql, kl, vl, lal, s_c)
    if reverse:
        o_c, o_l = jnp.flip(o_c, axis=2), jnp.flip(o_l, axis=2)
    return o_c, o_l


def _gla_inputs(p, lr, a_up_f, a_b_f, a_up_b, a_b_b, rope, dk, dv):
    hk, hv = dk // GLA_HEADS, dv // GLA_HEADS
    q = _to_heads(p[..., :dk].astype(F32), hk)
    k = _to_heads(p[..., dk:2 * dk].astype(F32), hk)
    v = _to_heads(p[..., 2 * dk:2 * dk + dv].astype(F32), hv)
    gg = p[..., 2 * dk + dv:2 * dk + 2 * dv].astype(F32)
    u = p[..., 2 * dk + 2 * dv:].astype(F32)
    lr_f, lr_b = lr[..., :GLA_RANK], lr[..., GLA_RANK:2 * GLA_RANK]
    la_f = _to_heads(jax.nn.log_sigmoid((lr_f @ a_up_f + a_b_f).astype(F32)) / GLA_GATE_NORM, hk)
    la_b = _to_heads(jax.nn.log_sigmoid((lr_b @ a_up_b + a_b_b).astype(F32)) / GLA_GATE_NORM, hk)
    if rope is not None:
        q = _apply_axial_rope(q, *rope)
        k = _apply_axial_rope(k, *rope)
    return q, k, v, la_f, la_b, gg, u


def _gla_output(o, gg, norm_g):
    bsz, hh, t, dv = o.shape
    o = o * lax.rsqrt(jnp.mean(o * o, axis=-1, keepdims=True) + EPS)
    o = o * norm_g.reshape(hh, 1, dv).astype(F32)
    o = o.transpose(0, 2, 1, 3).reshape(bsz, t, hh * dv).astype(gg.dtype)
    return o * jax.nn.silu(gg)


def _fourier_mix(u):
    bsz, t, fd = u.shape
    ug = u.reshape(bsz, t, FNET_GROUPS, fd // FNET_GROUPS).astype(F32)
    y = jnp.real(jnp.fft.fft2(ug, axes=(1, 3), norm='ortho'))
    return y.reshape(bsz, t, fd).astype(u.dtype)


def mixer_gla_fnet(x_ctx, x_lat, mod_ctx, mod_lat, norm_g1, w_in, w_out,
                   a_up_f, a_b_f, a_up_b, a_b_b, gla_norm_g, last):
    d = x_lat.shape[-1]
    dv = gla_norm_g.shape[0]
    dk = a_up_f.shape[1]
    t = x_lat.shape[1]
    hk = dk // GLA_HEADS
    n_main = 2 * dk + 2 * dv
    w_main = jnp.concatenate([w_in[:, :n_main], w_in[:, n_main + 2 * GLA_RANK:]], axis=1).astype(BF16)
    w_lr = jnp.pad(w_in[:, n_main:n_main + 2 * GLA_RANK], ((0, 0), (0, LANES - 2 * GLA_RANK)))
    cs = jnp.concatenate([jnp.full((dk,), hk ** -0.5, F32), jnp.ones((w_main.shape[1] - dk,), F32)])
    rope = _axial_rope_angles(t, hk)
    p_l, lr_l = norm_mod_matmul(x_lat, norm_g1, mod_lat[0], mod_lat[1], w_main, cs, w_aux=w_lr)
    p_c, lr_c = norm_mod_matmul(x_ctx, norm_g1, mod_ctx[0], mod_ctx[1], w_main, cs, w_aux=w_lr)
    qc, kc, vc, lafc, labc, gc, uc = _gla_inputs(p_c, lr_c, a_up_f, a_b_f, a_up_b, a_b_b, None, dk, dv)
    ql, kl, vl, lafl, labl, gl, ul = _gla_inputs(p_l, lr_l, a_up_f, a_b_f, a_up_b, a_b_b, rope, dk, dv)
    ofc, ofl = _gla_direction(qc, kc, vc, lafc, ql, kl, vl, lafl, False)
    obc, obl = _gla_direction(qc, kc, vc, labc, ql, kl, vl, labl, True)
    w_out_b = w_out.astype(BF16)
    a_l = jnp.concatenate([_gla_output(ofl + obl, gl, gla_norm_g), _fourier_mix(ul)], axis=-1).astype(BF16)
    x_lat = matmul_gated_residual(a_l, w_out_b, x_lat, mod_lat[2])
    if not last:
        a_c = jnp.concatenate([_gla_output(ofc + obc, gc, gla_norm_g), _fourier_mix(uc)], axis=-1).astype(BF16)
        x_ctx = matmul_gated_residual(a_c, w_out_b, x_ctx, mod_ctx[2])
    return x_ctx, x_lat


def mixer_neighbourhood(x_ctx, x_lat, mod_ctx, mod_lat, norm_g1, w_qkv, w_out, rpb, last):
    bsz, t, d = x_lat.shape
    l = x_ctx.shape[1]
    hd = d // NA_HEADS
    w_qkv_b = w_qkv.astype(BF16)
    w_out_b = w_out.astype(BF16)
    cs = jnp.concatenate([jnp.full((d,), hd ** -0.5, F32), jnp.ones((2 * d,), F32)])
    qkv_l = norm_mod_matmul(x_lat, norm_g1, mod_lat[0], mod_lat[1], w_qkv_b, cs)
    qkv_c = norm_mod_matmul(x_ctx, norm_g1, mod_ctx[0], mod_ctx[1], w_qkv_b, cs)
    o_l = neighbourhood_attention(qkv_l, qkv_c, rpb)
    x_lat = matmul_gated_residual(o_l, w_out_b, x_lat, mod_lat[2])
    if not last:
        q_c = qkv_c[..., :d].astype(F32).reshape(bsz, l, NA_HEADS, hd)
        k_c = qkv_c[..., d:2 * d].astype(F32).reshape(bsz, l, NA_HEADS, hd)
        v_c = qkv_c[..., 2 * d:].astype(F32).reshape(bsz, l, NA_HEADS, hd)
        s = jnp.einsum('bqhd,bkhd->bhqk', q_c, k_c)
        p = jax.nn.softmax(s.astype(F32), axis=-1)
        o_c = jnp.einsum('bhqk,bkhd->bqhd', p, v_c).reshape(bsz, l, d).astype(BF16)
        x_ctx = matmul_gated_residual(o_c, w_out_b, x_ctx, mod_ctx[2])
    return x_ctx, x_lat


def kernel(x, c, ctx, c_ctx, ada_w, ada_b, norm1_g, norm2_g, ab_w_in, ab_w_out, gla_a_up_f, gla_a_b_f,
           gla_a_up_b, gla_a_b_b, gla_norm_g, na_w_qkv, na_w_out, na_rpb, router_w, exp_w1, exp_w3,
           exp_w2, final_g):
    depth = ada_w.shape[0]
    bsz, t, d = x.shape
    cond = jnp.concatenate([c, c_ctx[None], jnp.zeros((8 - bsz - 1, d), F32)], axis=0)
    mods = ada_all_layers(cond, ada_w, ada_b)
    xl, xc = x, ctx
    for l in range(depth):
        last = l == depth - 1
        m = mods[l].reshape(8, N_MOD, d)
        mod_lat = [m[:bsz, i][:, None, :] for i in range(N_MOD)]
        mod_ctx = [jnp.broadcast_to(m[bsz, i][None, None, :], (bsz, 1, d)) for i in range(N_MOD)]
        if l % 2 == 0:
            e = l // 2
            xc, xl = mixer_gla_fnet(xc, xl, mod_ctx[:3], mod_lat[:3], norm1_g[l], ab_w_in[e], ab_w_out[e],
                                    gla_a_up_f[e], gla_a_b_f[e], gla_a_up_b[e], gla_a_b_b[e],
                                    gla_norm_g[e], last)
        else:
            o = l // 2
            xc, xl = mixer_neighbourhood(xc, xl, mod_ctx[:3], mod_lat[:3], norm1_g[l], na_w_qkv[o],
                                         na_w_out[o], na_rpb[o], last)
        w1, w3, w2 = exp_w1[l].astype(BF16), exp_w3[l].astype(BF16), exp_w2[l].astype(BF16)
        xl = xl + mod_lat[5] * expert_choice_moe(xl, norm2_g[l], mod_lat[3], mod_lat[4], router_w[l], w1, w3, w2)
        if not last:
            xc = xc + mod_ctx[5] * expert_choice_moe(xc, norm2_g[l], mod_ctx[3], mod_ctx[4], router_w[l],
                                                     w1, w3, w2)
    return rmsnorm_final(xl, final_g)
```

```python
import functools

import jax
import jax.numpy as jnp
import numpy as np
from jax import lax
from jax.experimental import pallas as pl
from jax.experimental.pallas import tpu as pltpu

F32 = jnp.float32
BF16 = jnp.bfloat16

GRID_W = 64
N_MOD = 6
EPS = 1e-6
GLA_HEADS = 4
GLA_RANK = 16
GLA_GATE_NORM = 16.0
GLA_CHUNK = 64
ROPE_BASE = 10000.0
FNET_GROUPS = 4
NA_HEADS = 16
NA_KR_MAX = 8
NA_KC = 16
N_EXPERTS = 16
EC_CAPACITY_FACTOR = 2

LANES = 128
VMEM_LIMIT_BYTES = 56 * 1024 * 1024
MASK_VALUE = -1e30

NA_ROWS_PER_BLOCK = 4
NA_HEADS_PER_STEP = 4


def _cparams(*sem):
    return pltpu.CompilerParams(dimension_semantics=sem, vmem_limit_bytes=VMEM_LIMIT_BYTES)


def _row_tile(t, cap):
    if t <= cap:
        return t
    for step in (LANES, 8):
        for tm in range(cap - cap % step, 0, -step):
            if t % tm == 0:
                return tm
    raise ValueError(f"no tile for {t} under {cap}")


def _ada_body(c_ref, w_ref, b_ref, o_ref):
    c = c_ref[...]
    a = (c * jax.nn.sigmoid(c)).astype(BF16)
    acc = jnp.dot(a, w_ref[0].astype(BF16), preferred_element_type=F32)
    o_ref[0] = acc + b_ref[0]


def ada_all_layers(cond, ada_w, ada_b):
    depth, d, n = ada_w.shape
    r = cond.shape[0]
    tn = _row_tile(n, 1024)
    return pl.pallas_call(
        _ada_body,
        grid=(depth, n // tn),
        in_specs=[
            pl.BlockSpec((r, d), lambda l, j: (0, 0)),
            pl.BlockSpec((1, d, tn), lambda l, j: (l, 0, j)),
            pl.BlockSpec((1, 1, tn), lambda l, j: (l, 0, j)),
        ],
        out_specs=pl.BlockSpec((1, r, tn), lambda l, j: (l, 0, j)),
        out_shape=jax.ShapeDtypeStruct((depth, r, n), F32),
        compiler_params=_cparams("parallel", "parallel"),
        name="ada_params",
    )(cond, ada_w, ada_b.reshape(depth, 1, n))


def _norm_mod(x, g, shift, scale):
    ms = jnp.mean(x * x, axis=-1, keepdims=True)
    y = x * lax.rsqrt(ms + EPS)
    return (y * g) * (1.0 + scale) + shift


def _nm_mm_body(x_ref, g_ref, sh_ref, sc_ref, w_ref, cs_ref, o_ref, h_scr):
    @pl.when(pl.program_id(2) == 0)
    def _():
        h = _norm_mod(x_ref[0], g_ref[...], sh_ref[0], sc_ref[0])
        h_scr[...] = h.astype(h_scr.dtype)

    acc = jnp.dot(h_scr[...], w_ref[...], preferred_element_type=F32)
    o_ref[0] = (acc * cs_ref[...]).astype(o_ref.dtype)


def _nm_mm_aux_body(x_ref, g_ref, sh_ref, sc_ref, w_ref, cs_ref, wa_ref, o_ref, oa_ref, h_scr):
    @pl.when(pl.program_id(2) == 0)
    def _():
        h = _norm_mod(x_ref[0], g_ref[...], sh_ref[0], sc_ref[0])
        h_scr[...] = h.astype(h_scr.dtype)
        oa_ref[0] = jnp.dot(h, wa_ref[...], preferred_element_type=F32,
                            precision=lax.Precision.HIGHEST)

    acc = jnp.dot(h_scr[...], w_ref[...], preferred_element_type=F32)
    o_ref[0] = (acc * cs_ref[...]).astype(o_ref.dtype)


def norm_mod_matmul(x, g, shift, scale, w, col_scale, *, out_dtype=BF16, w_aux=None, tm=512, tn=512):
    b, t, d = x.shape
    n = w.shape[1]
    tm = _row_tile(t, tm)
    tn = _row_tile(n, tn)
    in_specs = [
        pl.BlockSpec((1, tm, d), lambda bi, i, j: (bi, i, 0)),
        pl.BlockSpec((1, d), lambda bi, i, j: (0, 0)),
        pl.BlockSpec((1, 1, d), lambda bi, i, j: (bi, 0, 0)),
        pl.BlockSpec((1, 1, d), lambda bi, i, j: (bi, 0, 0)),
        pl.BlockSpec((d, tn), lambda bi, i, j: (0, j)),
        pl.BlockSpec((1, tn), lambda bi, i, j: (0, j)),
    ]
    args = [x, g.reshape(1, d), shift, scale, w, col_scale.reshape(1, n)]
    out_specs = pl.BlockSpec((1, tm, tn), lambda bi, i, j: (bi, i, j))
    out_shape = jax.ShapeDtypeStruct((b, t, n), out_dtype)
    body = _nm_mm_body
    if w_aux is not None:
        na = w_aux.shape[1]
        in_specs.append(pl.BlockSpec((d, na), lambda bi, i, j: (0, 0)))
        args.append(w_aux)
        out_specs = (out_specs, pl.BlockSpec((1, tm, na), lambda bi, i, j: (bi, i, 0)))
        out_shape = (out_shape, jax.ShapeDtypeStruct((b, t, na), F32))
        body = _nm_mm_aux_body
    return pl.pallas_call(
        body,
        grid=(b, t // tm, n // tn),
        in_specs=in_specs,
        out_specs=out_specs,
        out_shape=out_shape,
        scratch_shapes=[pltpu.VMEM((tm, d), BF16)],
        compiler_params=_cparams("parallel", "parallel", "arbitrary"),
        name="norm_mod_matmul",
    )(*args)


def _nm_router_body(x_ref, g_ref, sh_ref, sc_ref, rwt_ref, h_ref, lg_ref):
    h = _norm_mod(x_ref[0], g_ref[...], sh_ref[0], sc_ref[0])
    h_ref[0] = h.astype(h_ref.dtype)
    lg_ref[0] = lax.dot_general(rwt_ref[...], h, (((1,), (1,)), ((), ())),
                                preferred_element_type=F32, precision=lax.Precision.HIGHEST)


def norm_mod_router(x, g, shift, scale, router_w, *, tm=512):
    b, t, d = x.shape
    e = router_w.shape[1]
    tm = _row_tile(t, tm)
    return pl.pallas_call(
        _nm_router_body,
        grid=(b, t // tm),
        in_specs=[
            pl.BlockSpec((1, tm, d), lambda bi, i: (bi, i, 0)),
            pl.BlockSpec((1, d), lambda bi, i: (0, 0)),
            pl.BlockSpec((1, 1, d), lambda bi, i: (bi, 0, 0)),
            pl.BlockSpec((1, 1, d), lambda bi, i: (bi, 0, 0)),
            pl.BlockSpec((e, d), lambda bi, i: (0, 0)),
        ],
        out_specs=(
            pl.BlockSpec((1, tm, d), lambda bi, i: (bi, i, 0)),
            pl.BlockSpec((1, e, tm), lambda bi, i: (bi, 0, i)),
        ),
        out_shape=(jax.ShapeDtypeStruct((b, t, d), BF16), jax.ShapeDtypeStruct((b, e, t), F32)),
        compiler_params=_cparams("parallel", "parallel"),
        name="norm_mod_router",
    )(x, g.reshape(1, d), shift, scale, router_w.T)


def _route_body(lg_ref, u_ref, lb_ref, ui_ref, pos_ref, gs_ref, idx_ref, rs_ref, *, cap, t_valid):
    _, e, r, _ = lg_ref.shape
    lg = lg_ref[0]
    mx = jnp.max(lg, axis=0, keepdims=True)
    ex = jnp.exp(lg - mx)
    aff = ex / jnp.sum(ex, axis=0, keepdims=True)
    if t_valid < r * LANES:
        tok = (lax.broadcasted_iota(jnp.int32, (e, r, LANES), 1) * LANES
               + lax.broadcasted_iota(jnp.int32, (e, r, LANES), 2))
        aff = jnp.where(tok < t_valid, aff, -1.0)
    bits = lax.bitcast_convert_type(aff, jnp.int32)

    def count(mask):
        s1 = jnp.sum(jnp.where(mask, 1.0, 0.0), axis=1, keepdims=True)
        return jnp.sum(s1, axis=2, keepdims=True)

    def search(i, prefix):
        cand = prefix | lax.shift_left(jnp.int32(1), 30 - i)
        return jnp.where(count(bits >= cand) >= cap, cand, prefix)

    thr = lax.fori_loop(0, 31, search, jnp.zeros((e, 1, 1), jnp.int32))

    def cumsum_incl(mask):
        x2 = jnp.where(mask, 1.0, 0.0).reshape(e * r, LANES).astype(BF16)
        within = jnp.dot(x2, u_ref[...], preferred_element_type=F32)
        rowtot = jnp.broadcast_to(within[:, LANES - 1:LANES], (e * r, LANES)).astype(BF16)
        rowoff = jnp.dot(lb_ref[...], rowtot, preferred_element_type=F32)
        return within.reshape(e, r, LANES), rowoff.reshape(e, r, LANES)

    gt = bits > thr
    eq = bits == thr
    need = cap - count(gt)
    w_eq, o_eq = cumsum_incl(eq)
    sel = gt | (eq & (w_eq + o_eq <= need))
    within, rowoff = cumsum_incl(sel)
    pos = within + rowoff - 1.0
    pos_ref[0] = jnp.where(sel, pos, -1.0).astype(jnp.int32)
    gs_ref[0] = jnp.where(sel, aff, 0.0)
    rs_ref[0] = rowoff.astype(jnp.int32)

    s_col = lax.broadcasted_iota(jnp.int32, (cap, 1), 0).astype(F32)
    ones8 = jnp.ones((8, LANES), BF16)
    lane_r = lax.broadcasted_iota(jnp.int32, (cap, r), 1).astype(F32)
    for ei in range(e):
        sel_e = jnp.where(sel[ei], 1.0, 0.0).astype(BF16)
        rowtot_l = lax.dot_general(ones8, sel_e, (((1,), (1,)), ((), ())),
                                   preferred_element_type=F32)
        rowend_l = jnp.dot(rowtot_l.astype(BF16), ui_ref[...], preferred_element_type=F32)
        before = rowend_l[0:1, :] <= s_col
        row_s = jnp.sum(jnp.where(before, 1.0, 0.0), axis=1, keepdims=True)
        off_s = jnp.sum(jnp.where(before, rowtot_l[0:1, :], 0.0), axis=1, keepdims=True)
        onehot = jnp.where(lane_r == row_s, 1.0, 0.0).astype(BF16)
        w_rows = jnp.dot(onehot, within[ei].astype(BF16), preferred_element_type=F32)
        j_s = jnp.sum(jnp.where(w_rows <= s_col - off_s, 1.0, 0.0), axis=1, keepdims=True)
        idx_ref[0, ei] = jnp.broadcast_to(row_s * LANES + j_s, (cap, LANES)).astype(jnp.int32)


def route(logits, cap):
    b, e, t = logits.shape
    r = max(-(-t // LANES), 8)
    tp = r * LANES
    lg = jnp.pad(logits, ((0, 0), (0, 0), (0, tp - t))).reshape(b, e, r, LANES)
    tri = np.arange(LANES)[:, None] <= np.arange(LANES)[None, :]
    u = jnp.asarray(tri, BF16)
    ii = np.arange(e * r)
    lb = jnp.asarray((ii[:, None] // r == ii[None, :] // r) & (ii[None, :] % r < ii[:, None] % r), BF16)
    ui = jnp.asarray(np.arange(r)[:, None] <= np.arange(r)[None, :], BF16)
    blk = pl.BlockSpec((1, e, r, LANES), lambda bi: (bi, 0, 0, 0))
    pos, gs, idx, rs = pl.pallas_call(
        functools.partial(_route_body, cap=cap, t_valid=t),
        grid=(b,),
        in_specs=[blk,
                  pl.BlockSpec((LANES, LANES), lambda bi: (0, 0)),
                  pl.BlockSpec((e * r, e * r), lambda bi: (0, 0)),
                  pl.BlockSpec((r, r), lambda bi: (0, 0))],
        out_specs=(blk, blk, pl.BlockSpec((1, e, cap, LANES), lambda bi: (bi, 0, 0, 0)), blk),
        out_shape=(jax.ShapeDtypeStruct((b, e, r, LANES), jnp.int32),
                   jax.ShapeDtypeStruct((b, e, r, LANES), F32),
                   jax.ShapeDtypeStruct((b, e, cap, LANES), jnp.int32),
                   jax.ShapeDtypeStruct((b, e, r, LANES), jnp.int32)),
        compiler_params=_cparams("parallel"),
        name="route",
    )(lg, u, lb, ui)
    return (pos.reshape(b, e, tp)[:, :, :t], gs.reshape(b, e, tp)[:, :, :t], idx[..., 0], rs[..., 0])


def _mm_res_body(a_ref, w_ref, r_ref, g_ref, o_ref):
    acc = jnp.dot(a_ref[0], w_ref[...], preferred_element_type=F32)
    o_ref[0] = r_ref[0] + g_ref[0] * acc


def matmul_gated_residual(a, w, res, gate, *, tm=512, tn=512):
    b, t, k = a.shape
    n = w.shape[1]
    tm = _row_tile(t, tm)
    tn = _row_tile(n, tn)
    return pl.pallas_call(
        _mm_res_body,
        grid=(b, t // tm, n // tn),
        in_specs=[
            pl.BlockSpec((1, tm, k), lambda bi, i, j: (bi, i, 0)),
            pl.BlockSpec((k, tn), lambda bi, i, j: (0, j)),
            pl.BlockSpec((1, tm, tn), lambda bi, i, j: (bi, i, j)),
            pl.BlockSpec((1, 1, tn), lambda bi, i, j: (bi, 0, j)),
        ],
        out_specs=pl.BlockSpec((1, tm, tn), lambda bi, i, j: (bi, i, j)),
        out_shape=jax.ShapeDtypeStruct((b, t, n), F32),
        compiler_params=_cparams("parallel", "parallel", "parallel"),
        name="matmul_gated_residual",
    )(a, w, res, gate)


def _rmsnorm_body(x_ref, g_ref, o_ref):
    x = x_ref[0]
    ms = jnp.mean(x * x, axis=-1, keepdims=True)
    o_ref[0] = (x * lax.rsqrt(ms + EPS)) * g_ref[...]


def rmsnorm_final(x, g, *, tm=512):
    b, t, d = x.shape
    tm = _row_tile(t, tm)
    return pl.pallas_call(
        _rmsnorm_body,
        grid=(b, t // tm),
        in_specs=[pl.BlockSpec((1, tm, d), lambda bi, i: (bi, i, 0)),
                  pl.BlockSpec((1, d), lambda bi, i: (0, 0))],
        out_specs=pl.BlockSpec((1, tm, d), lambda bi, i: (bi, i, 0)),
        out_shape=jax.ShapeDtypeStruct((b, t, d), F32),
        compiler_params=_cparams("parallel", "parallel"),
        name="rmsnorm_final",
    )(x, g.reshape(1, d))


def _na_body(q_ref, k0_ref, k1_ref, k2_ref, v0_ref, v1_ref, v2_ref, kc_ref, vc_ref, bias_ref, o_ref,
             *, heads, hd, qb):
    dn = (((1,), (1,)), ((), ()))
    k_refs = (k0_ref, k1_ref, k2_ref)
    v_refs = (v0_ref, v1_ref, v2_ref)
    for g in range(heads):
        sl = slice(g * hd, (g + 1) * hd)
        q = q_ref[0, :, sl]
        s = [lax.dot_general(q, k_refs[i][0, :, sl], dn, preferred_element_type=F32)
             + bias_ref[0, g, :, i * qb:(i + 1) * qb] for i in range(3)]
        s.append(lax.dot_general(q, kc_ref[0, :, sl], dn, preferred_element_type=F32))
        m = functools.reduce(jnp.maximum, [jnp.max(si, axis=-1, keepdims=True) for si in s])
        p = [jnp.exp(si - m) for si in s]
        l = functools.reduce(jnp.add, [jnp.sum(pi, axis=-1, keepdims=True) for pi in p])
        vals = [v_refs[i][0, :, sl] for i in range(3)] + [vc_ref[0, :, sl]]
        o = functools.reduce(jnp.add, [jnp.dot(pi.astype(BF16), vi, preferred_element_type=F32)
                                       for pi, vi in zip(p, vals)])
        o_ref[0, :, sl] = (o / l).astype(o_ref.dtype)


def _na_bias_table(rpb, rows):
    r_blk = NA_ROWS_PER_BLOCK
    nblk = rows // r_blk
    kr = min(NA_KR_MAX, rows)
    tables = []
    for rb in (0, 1, nblk - 1):
        ks = min(max(rb - 1, 0), nblk - 3) * r_blk
        q_row = rb * r_blk + np.arange(r_blk)
        rs = np.clip(q_row - kr // 2, 0, rows - kr)
        k_row = ks + np.arange(3 * r_blk)
        ok_row = (k_row[None, :] >= rs[:, None]) & (k_row[None, :] < rs[:, None] + kr)
        rel_row = np.clip(k_row[None, :] - q_row[:, None] + (NA_KR_MAX - 1), 0, 2 * NA_KR_MAX - 2)
        q_col = np.arange(GRID_W)
        cs = np.clip(q_col - NA_KC // 2, 0, GRID_W - NA_KC)
        k_col = np.arange(GRID_W)
        ok_col = (k_col[None, :] >= cs[:, None]) & (k_col[None, :] < cs[:, None] + NA_KC)
        rel_col = np.clip(k_col[None, :] - q_col[:, None] + (NA_KC - 1), 0, 2 * NA_KC - 2)
        t = rpb[:, rel_row][:, :, :, rel_col]
        ok = ok_row[:, :, None, None] & ok_col[None, None, :, :]
        t = jnp.where(ok[None], t, MASK_VALUE)
        t = jnp.transpose(t, (0, 1, 3, 2, 4))
        tables.append(t.reshape(rpb.shape[0], r_blk * GRID_W, 3 * r_blk * GRID_W))
    return jnp.stack(tables)


def neighbourhood_attention(qkv, qkv_ctx, rpb):
    b, t, d3 = qkv.shape
    d = d3 // 3
    l = qkv_ctx.shape[1]
    hd = d // NA_HEADS
    rows = t // GRID_W
    r_blk = NA_ROWS_PER_BLOCK
    qb = r_blk * GRID_W
    nblk = rows // r_blk
    assert rows % r_blk == 0 and nblk >= 3 and min(NA_KR_MAX, rows) + r_blk <= 3 * r_blk
    g = NA_HEADS_PER_STEP
    gw = g * hd
    nhg = NA_HEADS // g
    bias = _na_bias_table(rpb.astype(F32), rows)

    def kstart(rb):
        return jnp.clip(rb - 1, 0, nblk - 3)

    def pattern(rb):
        return jnp.where(rb == 0, 0, jnp.where(rb == nblk - 1, 2, 1))

    q_spec = pl.BlockSpec((1, qb, gw), lambda hg, bi, rb: (bi, rb, hg))
    k_specs = [pl.BlockSpec((1, qb, gw), functools.partial(
        lambda hg, bi, rb, i: (bi, kstart(rb) + i, nhg + hg), i=i)) for i in range(3)]
    v_specs = [pl.BlockSpec((1, qb, gw), functools.partial(
        lambda hg, bi, rb, i: (bi, kstart(rb) + i, 2 * nhg + hg), i=i)) for i in range(3)]
    kc_spec = pl.BlockSpec((1, l, gw), lambda hg, bi, rb: (bi, 0, nhg + hg))
    vc_spec = pl.BlockSpec((1, l, gw), lambda hg, bi, rb: (bi, 0, 2 * nhg + hg))
    bias_spec = pl.BlockSpec((1, g, qb, 3 * qb), lambda hg, bi, rb: (pattern(rb), hg, 0, 0))
    return pl.pallas_call(
        functools.partial(_na_body, heads=g, hd=hd, qb=qb),
        grid=(nhg, b, nblk),
        in_specs=[q_spec] + k_specs + v_specs + [kc_spec, vc_spec, bias_spec],
        out_specs=pl.BlockSpec((1, qb, gw), lambda hg, bi, rb: (bi, rb, hg)),
        out_shape=jax.ShapeDtypeStruct((b, t, d), BF16),
        compiler_params=_cparams("parallel", "parallel", "arbitrary"),
        name="neighbourhood_attention",
    )(qkv, qkv, qkv, qkv, qkv, qkv, qkv, qkv_ctx, qkv_ctx, bias)


def _moe_ffn_body(x_ref, w1_ref, w3_ref, w2_ref, o_ref):
    x = x_ref[0]
    a = jnp.dot(x, w1_ref[0], preferred_element_type=F32)
    bb = jnp.dot(x, w3_ref[0], preferred_element_type=F32)
    hm = (a * jax.nn.sigmoid(a) * bb).astype(BF16)
    o_ref[0] = jnp.dot(hm, w2_ref[0], preferred_element_type=F32).astype(o_ref.dtype)


def moe_ffn(xe, w1, w3, w2, *, tm=256):
    e, m, d = xe.shape
    f = w1.shape[2]
    tm = _row_tile(m, tm)
    return pl.pallas_call(
        _moe_ffn_body,
        grid=(e, m // tm),
        in_specs=[
            pl.BlockSpec((1, tm, d), lambda ei, i: (ei, i, 0)),
            pl.BlockSpec((1, d, f), lambda ei, i: (ei, 0, 0)),
            pl.BlockSpec((1, d, f), lambda ei, i: (ei, 0, 0)),
            pl.BlockSpec((1, f, d), lambda ei, i: (ei, 0, 0)),
        ],
        out_specs=pl.BlockSpec((1, tm, d), lambda ei, i: (ei, i, 0)),
        out_shape=jax.ShapeDtypeStruct((e, m, d), BF16),
        compiler_params=_cparams("parallel", "arbitrary"),
        name="moe_ffn",
    )(xe, w1, w3, w2)


COMBINE_TOKENS = 256
COMBINE_WINDOW = 64
BF16_SUBLANES = 16


def _combine_body(ts_ref, pos_ref, gs_ref, x_ref, g2_ref, ye_hbm, o_ref, ybuf, yextra, sem, sem_x,
                  *, n_exp, cap, win, n_batch):
    bi = pl.program_id(0)
    ti = pl.program_id(1)
    nt = pl.num_programs(1)
    step = bi * nt + ti
    total = n_batch * nt
    slot = lax.rem(step, 2)
    tt = pos_ref.shape[2]

    def tile_start(b, t, e):
        return ts_ref[(b * (nt + 1) + t) * n_exp + e]

    def window_start(b, t, e, w):
        st = tile_start(b, t, e)
        a = lax.shift_left(lax.shift_right_logical(st, 4), 4) + w * win
        return jnp.minimum(a, cap - win)

    def window_copy(b, e, a, dst, dsem):
        row0 = pl.multiple_of((e * n_batch + b) * cap + a, BF16_SUBLANES)
        return pltpu.make_async_copy(ye_hbm.at[pl.ds(row0, win)], dst, dsem)

    def issue(b, t, sl):
        for e in range(n_exp):
            window_copy(b, e, window_start(b, t, e, 0), ybuf.at[sl, e], sem.at[sl]).start()

    @pl.when(step == 0)
    def _():
        issue(bi, ti, slot)

    for e in range(n_exp):
        window_copy(bi, e, 0, ybuf.at[slot, e], sem.at[slot]).wait()

    @pl.when(step + 1 < total)
    def _():
        nxt = step + 1
        issue(nxt // nt, lax.rem(nxt, nt), 1 - slot)

    m_iota = lax.broadcasted_iota(jnp.int32, (win, tt), 0)
    dn = (((0,), (0,)), ((), ()))

    def gather_matrix(w):
        rows = []
        for e in range(n_exp):
            a = window_start(bi, ti, e, w)
            lo = window_start(bi, ti, e, 0) + w * win
            p = pos_ref[0, e:e + 1, :]
            hit = (p - a == m_iota) & (p >= lo)
            rows.append(jnp.where(hit, gs_ref[0, e:e + 1, :], 0.0).astype(BF16))
        return jnp.concatenate(rows, axis=0)

    d = ybuf.shape[-1]
    acc = lax.dot_general(gather_matrix(0), ybuf[slot].reshape(n_exp * win, d), dn,
                          preferred_element_type=F32)

    n_win = jnp.int32(1)
    for e in range(n_exp):
        span = tile_start(bi, ti + 1, e) - window_start(bi, ti, e, 0)
        n_win = jnp.maximum(n_win, (span + win - 1) // win)

    def extra(w, acc):
        for e in range(n_exp):
            cp = window_copy(bi, e, window_start(bi, ti, e, w), yextra.at[e], sem_x)
            cp.start()
            cp.wait()
        return acc + lax.dot_general(gather_matrix(w), yextra[...].reshape(n_exp * win, d), dn,
                                     preferred_element_type=F32)

    acc = lax.fori_loop(1, n_win, extra, acc)
    o_ref[0] = x_ref[0] + g2_ref[0] * acc


def moe_combine(x, g2, ye, pos, gs, tile_start, cap):
    b, t, d = x.shape
    e = pos.shape[1]
    tt = min(COMBINE_TOKENS, t)
    win = min(COMBINE_WINDOW, cap)
    nt = t // tt
    grid_spec = pltpu.PrefetchScalarGridSpec(
        num_scalar_prefetch=1,
        grid=(b, nt),
        in_specs=[
            pl.BlockSpec((1, e, tt), lambda bi, ti, ts: (bi, 0, ti)),
            pl.BlockSpec((1, e, tt), lambda bi, ti, ts: (bi, 0, ti)),
            pl.BlockSpec((1, tt, d), lambda bi, ti, ts: (bi, ti, 0)),
            pl.BlockSpec((1, 1, d), lambda bi, ti, ts: (bi, 0, 0)),
            pl.BlockSpec(memory_space=pl.ANY),
        ],
        out_specs=pl.BlockSpec((1, tt, d), lambda bi, ti, ts: (bi, ti, 0)),
        scratch_shapes=[
            pltpu.VMEM((2, e, win, d), BF16),
            pltpu.VMEM((e, win, d), BF16),
            pltpu.SemaphoreType.DMA((2,)),
            pltpu.SemaphoreType.DMA(()),
        ],
    )
    return pl.pallas_call(
        functools.partial(_combine_body, n_exp=e, cap=cap, win=win, n_batch=b),
        grid_spec=grid_spec,
        out_shape=jax.ShapeDtypeStruct((b, t, d), F32),
        compiler_params=_cparams("arbitrary", "arbitrary"),
        name="moe_combine",
    )(tile_start, pos, gs, x, g2, ye)


def expert_choice_moe(x, g, shift, scale, g2, router_w, w1, w3, w2):
    b, t, d = x.shape
    e = router_w.shape[1]
    cap = EC_CAPACITY_FACTOR * t // e
    h, logits = norm_mod_router(x, g, shift, scale, router_w)
    pos, gs, idx, rowstart = route(logits, cap)
    xe = jax.vmap(lambda hb, ib: hb[ib])(h, idx)
    xe = jnp.swapaxes(xe, 0, 1).reshape(e, b * cap, d)
    ye = moe_ffn(xe, w1, w3, w2).reshape(e * b * cap, d)
    tt = min(COMBINE_TOKENS, t)
    ts = rowstart[:, :, ::tt // LANES][:, :, :t // tt]
    ts = jnp.concatenate([jnp.swapaxes(ts, 1, 2), jnp.full((b, 1, e), cap, jnp.int32)], axis=1)
    return moe_combine(x, g2, ye, pos, gs, ts.reshape(-1), cap)


def _to_heads(a, dh):
    b, t = a.shape[:2]
    return a.reshape(b, t, -1, dh).transpose(0, 2, 1, 3)


def _axial_rope_angles(n_tok, head_dim):
    n_freq = head_dim // 4
    inv = ROPE_BASE ** (-jnp.arange(n_freq, dtype=F32) / n_freq)
    tt = jnp.arange(n_tok)
    row = (tt // GRID_W).astype(F32)
    col = (tt % GRID_W).astype(F32)
    return row[:, None] * inv[None], col[:, None] * inv[None]


def _rotate(x, ang):
    m = ang.shape[-1]
    cos = jnp.cos(ang).astype(x.dtype)
    sin = jnp.sin(ang).astype(x.dtype)
    x1, x2 = x[..., :m], x[..., m:]
    return jnp.concatenate([x1 * cos - x2 * sin, x1 * sin + x2 * cos], axis=-1)


def _apply_axial_rope(x, ang_row, ang_col):
    half = x.shape[-1] // 2
    return jnp.concatenate([_rotate(x[..., :half], ang_row), _rotate(x[..., half:], ang_col)], axis=-1)


def _gla_chunked(q, k, v, log_a, s0):
    bsz, hh, t, dk = q.shape
    dv = v.shape[-1]
    n, c = t // GLA_CHUNK, GLA_CHUNK
    q = q.astype(F32).reshape(bsz, hh, n, c, dk)
    k = k.astype(F32).reshape(bsz, hh, n, c, dk)
    v = v.astype(F32).reshape(bsz, hh, n, c, dv)
    bcs = jnp.cumsum(log_a.astype(F32).reshape(bsz, hh, n, c, dk), axis=3)
    b_last = bcs[:, :, :, -1:, :]
    qe = q * jnp.exp(bcs)
    ke = k * jnp.exp(-bcs)
    kd = k * jnp.exp(b_last - bcs)
    mask = jnp.tril(jnp.ones((c, c), dtype=bool))
    att = jnp.where(mask, jnp.einsum('bhnid,bhnjd->bhnij', qe, ke), 0.0)
    o_intra = jnp.einsum('bhnij,bhnje->bhnie', att, v)
    u = jnp.einsum('bhnjd,bhnje->bhnde', kd, v)
    decay = jnp.exp(b_last[:, :, :, 0, :])

    def step(s, xs):
        dec, uc = xs
        return dec[..., None] * s + uc, s

    s_fin, s_in = lax.scan(step, s0.astype(F32), (jnp.moveaxis(decay, 2, 0), jnp.moveaxis(u, 2, 0)))
    s_in = jnp.moveaxis(s_in, 0, 2)
    o = o_intra + jnp.einsum('bhnid,bhnde->bhnie', qe, s_in)
    return o.reshape(bsz, hh, t, dv), s_fin


def _gla_direction(qc, kc, vc, lac, ql, kl, vl, lal, reverse):
    if reverse:
        qc, kc, vc, lac, ql, kl, vl, lal = [jnp.flip(a, axis=2) for a in (qc, kc, vc, lac, ql, kl, vl, lal)]
    bsz, hh, _, dk = qc.shape
    s0 = jnp.zeros((bsz, hh, dk, vc.shape[-1]), F32)
    o_c, s_c = _gla_chunked(qc, kc, vc, lac, s0)
    o_l, _ = _gla_chunked(ql, kl, vl, lal, s_c)
    if reverse:
        o_c, o_l = jnp.flip(o_c, axis=2), jnp.flip(o_l, axis=2)
    return o_c, o_l


def _gla_inputs(p, lr, a_up_f, a_b_f, a_up_b, a_b_b, rope, dk, dv):
    hk, hv = dk // GLA_HEADS, dv // GLA_HEADS
    q = _to_heads(p[..., :dk].astype(F32), hk)
    k = _to_heads(p[..., dk:2 * dk].astype(F32), hk)
    v = _to_heads(p[..., 2 * dk:2 * dk + dv].astype(F32), hv)
    gg = p[..., 2 * dk + dv:2 * dk + 2 * dv].astype(F32)
    u = p[..., 2 * dk + 2 * dv:].astype(F32)
    lr_f, lr_b = lr[..., :GLA_RANK], lr[..., GLA_RANK:2 * GLA_RANK]
    la_f = _to_heads(jax.nn.log_sigmoid((lr_f @ a_up_f + a_b_f).astype(F32)) / GLA_GATE_NORM, hk)
    la_b = _to_heads(jax.nn.log_sigmoid((lr_b @ a_up_b + a_b_b).astype(F32)) / GLA_GATE_NORM, hk)
    if rope is not None:
        q = _apply_axial_rope(q, *rope)
        k = _apply_axial_rope(k, *rope)
    return q, k, v, la_f, la_b, gg, u


def _gla_output(o, gg, norm_g):
    bsz, hh, t, dv = o.shape
    o = o * lax.rsqrt(jnp.mean(o * o, axis=-1, keepdims=True) + EPS)
    o = o * norm_g.reshape(hh, 1, dv).astype(F32)
    o = o.transpose(0, 2, 1, 3).reshape(bsz, t, hh * dv).astype(gg.dtype)
    return o * jax.nn.silu(gg)


def _fourier_mix(u):
    bsz, t, fd = u.shape
    ug = u.reshape(bsz, t, FNET_GROUPS, fd // FNET_GROUPS).astype(F32)
    y = jnp.real(jnp.fft.fft2(ug, axes=(1, 3), norm='ortho'))
    return y.reshape(bsz, t, fd).astype(u.dtype)


def mixer_gla_fnet(x_ctx, x_lat, mod_ctx, mod_lat, norm_g1, w_in, w_out,
                   a_up_f, a_b_f, a_up_b, a_b_b, gla_norm_g, last):
    d = x_lat.shape[-1]
    dv = gla_norm_g.shape[0]
    dk = a_up_f.shape[1]
    t = x_lat.shape[1]
    hk = dk // GLA_HEADS
    n_main = 2 * dk + 2 * dv
    w_main = jnp.concatenate([w_in[:, :n_main], w_in[:, n_main + 2 * GLA_RANK:]], axis=1).astype(BF16)
    w_lr = jnp.pad(w_in[:, n_main:n_main + 2 * GLA_RANK], ((0, 0), (0, LANES - 2 * GLA_RANK)))
    cs = jnp.concatenate([jnp.full((dk,), hk ** -0.5, F32), jnp.ones((w_main.shape[1] - dk,), F32)])
    rope = _axial_rope_angles(t, hk)
    p_l, lr_l = norm_mod_matmul(x_lat, norm_g1, mod_lat[0], mod_lat[1], w_main, cs, w_aux=w_lr)
    p_c, lr_c = norm_mod_matmul(x_ctx, norm_g1, mod_ctx[0], mod_ctx[1], w_main, cs, w_aux=w_lr)
    qc, kc, vc, lafc, labc, gc, uc = _gla_inputs(p_c, lr_c, a_up_f, a_b_f, a_up_b, a_b_b, None, dk, dv)
    ql, kl, vl, lafl, labl, gl, ul = _gla_inputs(p_l, lr_l, a_up_f, a_b_f, a_up_b, a_b_b, rope, dk, dv)
    ofc, ofl = _gla_direction(qc, kc, vc, lafc, ql, kl, vl, lafl, False)
    obc, obl = _gla_direction(qc, kc, vc, labc, ql, kl, vl, labl, True)
    w_out_b = w_out.astype(BF16)
    a_l = jnp.concatenate([_gla_output(ofl + obl, gl, gla_norm_g), _fourier_mix(ul)], axis=-1).astype(BF16)
    x_lat = matmul_gated_residual(a_l, w_out_b, x_lat, mod_lat[2])
    if not last:
        a_c = jnp.concatenate([_gla_output(ofc + obc, gc, gla_norm_g), _fourier_mix(uc)], axis=-1).astype(BF16)
        x_ctx = matmul_gated_residual(a_c, w_out_b, x_ctx, mod_ctx[2])
    return x_ctx, x_lat


def mixer_neighbourhood(x_ctx, x_lat, mod_ctx, mod_lat, norm_g1, w_qkv, w_out, rpb, last):
    bsz, t, d = x_lat.shape
    l = x_ctx.shape[1]
    hd = d // NA_HEADS
    w_qkv_b = w_qkv.astype(BF16)
    w_out_b = w_out.astype(BF16)
    cs = jnp.concatenate([jnp.full((d,), hd ** -0.5, F32), jnp.ones((2 * d,), F32)])
    qkv_l = norm_mod_matmul(x_lat, norm_g1, mod_lat[0], mod_lat[1], w_qkv_b, cs)
    qkv_c = norm_mod_matmul(x_ctx, norm_g1, mod_ctx[0], mod_ctx[1], w_qkv_b, cs)
    o_l = neighbourhood_attention(qkv_l, qkv_c, rpb)
    x_lat = matmul_gated_residual(o_l, w_out_b, x_lat, mod_lat[2])
    if not last:
        q_c = qkv_c[..., :d].astype(F32).reshape(bsz, l, NA_HEADS, hd)
        k_c = qkv_c[..., d:2 * d].astype(F32).reshape(bsz, l, NA_HEADS, hd)
        v_c = qkv_c[..., 2 * d:].astype(F32).reshape(bsz, l, NA_HEADS, hd)
        s = jnp.einsum('bqhd,bkhd->bhqk', q_c, k_c)
        p = jax.nn.softmax(s.astype(F32), axis=-1)
        o_c = jnp.einsum('bhqk,bkhd->bqhd', p, v_c).reshape(bsz, l, d).astype(BF16)
        x_ctx = matmul_gated_residual(o_c, w_out_b, x_ctx, mod_ctx[2])
    return x_ctx, x_lat


def kernel(x, c, ctx, c_ctx, ada_w, ada_b, norm1_g, norm2_g, ab_w_in, ab_w_out, gla_a_up_f, gla_a_b_f,
           gla_a_up_b, gla_a_b_b, gla_norm_g, na_w_qkv, na_w_out, na_rpb, router_w, exp_w1, exp_w3,
           exp_w2, final_g):
    depth = ada_w.shape[0]
    bsz, t, d = x.shape
    cond = jnp.concatenate([c, c_ctx[None], jnp.zeros((8 - bsz - 1, d), F32)], axis=0)
    mods = ada_all_layers(cond, ada_w, ada_b)
    xl, xc = x, ctx
    for l in range(depth):
        last = l == depth - 1
        m = mods[l].reshape(8, N_MOD, d)
        mod_lat = [m[:bsz, i][:, None, :] for i in range(N_MOD)]
        mod_ctx = [jnp.broadcast_to(m[bsz, i][None, None, :], (bsz, 1, d)) for i in range(N_MOD)]
        if l % 2 == 0:
            e = l // 2
            xc, xl = mixer_gla_fnet(xc, xl, mod_ctx[:3], mod_lat[:3], norm1_g[l], ab_w_in[e], ab_w_out[e],
                                    gla_a_up_f[e], gla_a_b_f[e], gla_a_up_b[e], gla_a_b_b[e],
                                    gla_norm_g[e], last)
        else:
            o = l // 2
            xc, xl = mixer_neighbourhood(xc, xl, mod_ctx[:3], mod_lat[:3], norm1_g[l], na_w_qkv[o],
                                         na_w_out[o], na_rpb[o], last)
        w1, w3, w2 = exp_w1[l].astype(BF16), exp_w3[l].astype(BF16), exp_w2[l].astype(BF16)
        xl = expert_choice_moe(xl, norm2_g[l], mod_lat[3], mod_lat[4], mod_lat[5], router_w[l], w1, w3, w2)
        if not last:
            xc = expert_choice_moe(xc, norm2_g[l], mod_ctx[3], mod_ctx[4], mod_ctx[5], router_w[l], w1, w3, w2)
    return rmsnorm_final(xl, final_g)
```

```python
import functools

import jax
import jax.numpy as jnp
import numpy as np
from jax import lax
from jax.experimental import pallas as pl
from jax.experimental.pallas import tpu as pltpu

F32 = jnp.float32
BF16 = jnp.bfloat16

GRID_W = 64
N_MOD = 6
EPS = 1e-6
GLA_HEADS = 4
GLA_RANK = 16
GLA_GATE_NORM = 16.0
GLA_CHUNK = 64
ROPE_BASE = 10000.0
FNET_GROUPS = 4
NA_HEADS = 16
NA_KR_MAX = 8
NA_KC = 16
N_EXPERTS = 16
EC_CAPACITY_FACTOR = 2

LANES = 128
VMEM_LIMIT_BYTES = 56 * 1024 * 1024
MASK_VALUE = -1e30

NA_ROWS_PER_BLOCK = 4
NA_HEADS_PER_STEP = 4


def _cparams(*sem):
    return pltpu.CompilerParams(dimension_semantics=sem, vmem_limit_bytes=VMEM_LIMIT_BYTES)


def _row_tile(t, cap):
    if t <= cap:
        return t
    for step in (LANES, 8):
        for tm in range(cap - cap % step, 0, -step):
            if t % tm == 0:
                return tm
    raise ValueError(f"no tile for {t} under {cap}")


def _ada_body(c_ref, w_ref, b_ref, o_ref):
    c = c_ref[...]
    a = (c * jax.nn.sigmoid(c)).astype(BF16)
    acc = jnp.dot(a, w_ref[0].astype(BF16), preferred_element_type=F32)
    o_ref[0] = acc + b_ref[0]


def ada_all_layers(cond, ada_w, ada_b):
    depth, d, n = ada_w.shape
    r = cond.shape[0]
    tn = _row_tile(n, 1024)
    return pl.pallas_call(
        _ada_body,
        grid=(depth, n // tn),
        in_specs=[
            pl.BlockSpec((r, d), lambda l, j: (0, 0)),
            pl.BlockSpec((1, d, tn), lambda l, j: (l, 0, j)),
            pl.BlockSpec((1, 1, tn), lambda l, j: (l, 0, j)),
        ],
        out_specs=pl.BlockSpec((1, r, tn), lambda l, j: (l, 0, j)),
        out_shape=jax.ShapeDtypeStruct((depth, r, n), F32),
        compiler_params=_cparams("parallel", "parallel"),
        name="ada_params",
    )(cond, ada_w, ada_b.reshape(depth, 1, n))


def _norm_mod(x, g, shift, scale):
    ms = jnp.mean(x * x, axis=-1, keepdims=True)
    y = x * lax.rsqrt(ms + EPS)
    return (y * g) * (1.0 + scale) + shift


def _nm_mm_body(x_ref, g_ref, sh_ref, sc_ref, w_ref, cs_ref, o_ref, h_scr):
    @pl.when(pl.program_id(2) == 0)
    def _():
        h = _norm_mod(x_ref[0], g_ref[...], sh_ref[0], sc_ref[0])
        h_scr[...] = h.astype(h_scr.dtype)

    acc = jnp.dot(h_scr[...], w_ref[...], preferred_element_type=F32)
    o_ref[0] = (acc * cs_ref[...]).astype(o_ref.dtype)


def _nm_mm_aux_body(x_ref, g_ref, sh_ref, sc_ref, w_ref, cs_ref, wa_ref, o_ref, oa_ref, h_scr):
    @pl.when(pl.program_id(2) == 0)
    def _():
        h = _norm_mod(x_ref[0], g_ref[...], sh_ref[0], sc_ref[0])
        h_scr[...] = h.astype(h_scr.dtype)
        oa_ref[0] = jnp.dot(h, wa_ref[...], preferred_element_type=F32,
                            precision=lax.Precision.HIGHEST)

    acc = jnp.dot(h_scr[...], w_ref[...], preferred_element_type=F32)
    o_ref[0] = (acc * cs_ref[...]).astype(o_ref.dtype)


def norm_mod_matmul(x, g, shift, scale, w, col_scale, *, out_dtype=BF16, w_aux=None, tm=512, tn=2048):
    b, t, d = x.shape
    n = w.shape[1]
    tm = _row_tile(t, tm)
    tn = _row_tile(n, tn)
    in_specs = [
        pl.BlockSpec((1, tm, d), lambda bi, i, j: (bi, i, 0)),
        pl.BlockSpec((1, d), lambda bi, i, j: (0, 0)),
        pl.BlockSpec((1, 1, d), lambda bi, i, j: (bi, 0, 0)),
        pl.BlockSpec((1, 1, d), lambda bi, i, j: (bi, 0, 0)),
        pl.BlockSpec((d, tn), lambda bi, i, j: (0, j)),
        pl.BlockSpec((1, tn), lambda bi, i, j: (0, j)),
    ]
    args = [x, g.reshape(1, d), shift, scale, w, col_scale.reshape(1, n)]
    out_specs = pl.BlockSpec((1, tm, tn), lambda bi, i, j: (bi, i, j))
    out_shape = jax.ShapeDtypeStruct((b, t, n), out_dtype)
    body = _nm_mm_body
    if w_aux is not None:
        na = w_aux.shape[1]
        in_specs.append(pl.BlockSpec((d, na), lambda bi, i, j: (0, 0)))
        args.append(w_aux)
        out_specs = (out_specs, pl.BlockSpec((1, tm, na), lambda bi, i, j: (bi, i, 0)))
        out_shape = (out_shape, jax.ShapeDtypeStruct((b, t, na), F32))
        body = _nm_mm_aux_body
    return pl.pallas_call(
        body,
        grid=(b, t // tm, n // tn),
        in_specs=in_specs,
        out_specs=out_specs,
        out_shape=out_shape,
        scratch_shapes=[pltpu.VMEM((tm, d), BF16)],
        compiler_params=_cparams("parallel", "parallel", "arbitrary"),
        name="norm_mod_matmul",
    )(*args)


def _nm_router_body(x_ref, g_ref, sh_ref, sc_ref, rwt_ref, h_ref, lg_ref):
    h = _norm_mod(x_ref[0], g_ref[...], sh_ref[0], sc_ref[0])
    h_ref[0] = h.astype(h_ref.dtype)
    lg_ref[0] = lax.dot_general(rwt_ref[...], h, (((1,), (1,)), ((), ())),
                                preferred_element_type=F32, precision=lax.Precision.HIGHEST)


def norm_mod_router(x, g, shift, scale, router_w, *, tm=512):
    b, t, d = x.shape
    e = router_w.shape[1]
    tm = _row_tile(t, tm)
    return pl.pallas_call(
        _nm_router_body,
        grid=(b, t // tm),
        in_specs=[
            pl.BlockSpec((1, tm, d), lambda bi, i: (bi, i, 0)),
            pl.BlockSpec((1, d), lambda bi, i: (0, 0)),
            pl.BlockSpec((1, 1, d), lambda bi, i: (bi, 0, 0)),
            pl.BlockSpec((1, 1, d), lambda bi, i: (bi, 0, 0)),
            pl.BlockSpec((e, d), lambda bi, i: (0, 0)),
        ],
        out_specs=(
            pl.BlockSpec((1, tm, d), lambda bi, i: (bi, i, 0)),
            pl.BlockSpec((1, e, tm), lambda bi, i: (bi, 0, i)),
        ),
        out_shape=(jax.ShapeDtypeStruct((b, t, d), F32), jax.ShapeDtypeStruct((b, e, t), F32)),
        compiler_params=_cparams("parallel", "parallel"),
        name="norm_mod_router",
    )(x, g.reshape(1, d), shift, scale, router_w.T)


def _route_body(lg_ref, u_ref, lb_ref, ui_ref, pos_ref, gs_ref, idx_ref, rs_ref, *, cap, t_valid):
    _, e, r, _ = lg_ref.shape
    lg = lg_ref[0]
    mx = jnp.max(lg, axis=0, keepdims=True)
    ex = jnp.exp(lg - mx)
    aff = ex / jnp.sum(ex, axis=0, keepdims=True)
    if t_valid < r * LANES:
        tok = (lax.broadcasted_iota(jnp.int32, (e, r, LANES), 1) * LANES
               + lax.broadcasted_iota(jnp.int32, (e, r, LANES), 2))
        aff = jnp.where(tok < t_valid, aff, -1.0)
    bits = lax.bitcast_convert_type(aff, jnp.int32)

    def count(mask):
        s1 = jnp.sum(jnp.where(mask, 1.0, 0.0), axis=1, keepdims=True)
        return jnp.sum(s1, axis=2, keepdims=True)

    def search(i, prefix):
        cand = prefix | lax.shift_left(jnp.int32(1), 30 - i)
        return jnp.where(count(bits >= cand) >= cap, cand, prefix)

    thr = lax.fori_loop(0, 31, search, jnp.zeros((e, 1, 1), jnp.int32))

    def cumsum_incl(mask):
        x2 = jnp.where(mask, 1.0, 0.0).reshape(e * r, LANES).astype(BF16)
        within = jnp.dot(x2, u_ref[...], preferred_element_type=F32)
        rowtot = jnp.broadcast_to(within[:, LANES - 1:LANES], (e * r, LANES)).astype(BF16)
        rowoff = jnp.dot(lb_ref[...], rowtot, preferred_element_type=F32)
        return within.reshape(e, r, LANES), rowoff.reshape(e, r, LANES)

    gt = bits > thr
    eq = bits == thr
    need = cap - count(gt)
    w_eq, o_eq = cumsum_incl(eq)
    sel = gt | (eq & (w_eq + o_eq <= need))
    within, rowoff = cumsum_incl(sel)
    pos = within + rowoff - 1.0
    pos_ref[0] = jnp.where(sel, pos, -1.0).astype(jnp.int32)
    gs_ref[0] = jnp.where(sel, aff, 0.0)
    rs_ref[0] = rowoff.astype(jnp.int32)

    s_col = lax.broadcasted_iota(jnp.int32, (cap, 1), 0).astype(F32)
    ones8 = jnp.ones((8, LANES), BF16)
    lane_r = lax.broadcasted_iota(jnp.int32, (cap, r), 1).astype(F32)
    for ei in range(e):
        sel_e = jnp.where(sel[ei], 1.0, 0.0).astype(BF16)
        rowtot_l = lax.dot_general(ones8, sel_e, (((1,), (1,)), ((), ())),
                                   preferred_element_type=F32)
        rowend_l = jnp.dot(rowtot_l.astype(BF16), ui_ref[...], preferred_element_type=F32)
        before = rowend_l[0:1, :] <= s_col
        row_s = jnp.sum(jnp.where(before, 1.0, 0.0), axis=1, keepdims=True)
        off_s = jnp.sum(jnp.where(before, rowtot_l[0:1, :], 0.0), axis=1, keepdims=True)
        onehot = jnp.where(lane_r == row_s, 1.0, 0.0).astype(BF16)
        w_rows = jnp.dot(onehot, within[ei].astype(BF16), preferred_element_type=F32)
        j_s = jnp.sum(jnp.where(w_rows <= s_col - off_s, 1.0, 0.0), axis=1, keepdims=True)
        idx_ref[0, ei] = jnp.broadcast_to(row_s * LANES + j_s, (cap, LANES)).astype(jnp.int32)


def route(logits, cap):
    b, e, t = logits.shape
    r = max(-(-t // LANES), 8)
    tp = r * LANES
    lg = jnp.pad(logits, ((0, 0), (0, 0), (0, tp - t))).reshape(b, e, r, LANES)
    tri = np.arange(LANES)[:, None] <= np.arange(LANES)[None, :]
    u = jnp.asarray(tri, BF16)
    ii = np.arange(e * r)
    lb = jnp.asarray((ii[:, None] // r == ii[None, :] // r) & (ii[None, :] % r < ii[:, None] % r), BF16)
    ui = jnp.asarray(np.arange(r)[:, None] <= np.arange(r)[None, :], BF16)
    blk = pl.BlockSpec((1, e, r, LANES), lambda bi: (bi, 0, 0, 0))
    pos, gs, idx, rs = pl.pallas_call(
        functools.partial(_route_body, cap=cap, t_valid=t),
        grid=(b,),
        in_specs=[blk,
                  pl.BlockSpec((LANES, LANES), lambda bi: (0, 0)),
                  pl.BlockSpec((e * r, e * r), lambda bi: (0, 0)),
                  pl.BlockSpec((r, r), lambda bi: (0, 0))],
        out_specs=(blk, blk, pl.BlockSpec((1, e, cap, LANES), lambda bi: (bi, 0, 0, 0)), blk),
        out_shape=(jax.ShapeDtypeStruct((b, e, r, LANES), jnp.int32),
                   jax.ShapeDtypeStruct((b, e, r, LANES), F32),
                   jax.ShapeDtypeStruct((b, e, cap, LANES), jnp.int32),
                   jax.ShapeDtypeStruct((b, e, r, LANES), jnp.int32)),
        compiler_params=_cparams("parallel"),
        name="route",
    )(lg, u, lb, ui)
    return (pos.reshape(b, e, tp)[:, :, :t], gs.reshape(b, e, tp)[:, :, :t], idx[..., 0], rs[..., 0])


def _mm_res_body(a_ref, w_ref, r_ref, g_ref, o_ref):
    acc = jnp.dot(a_ref[0], w_ref[...], preferred_element_type=F32)
    o_ref[0] = r_ref[0] + g_ref[0] * acc


def matmul_gated_residual(a, w, res, gate, *, tm=512, tn=2048):
    b, t, k = a.shape
    n = w.shape[1]
    tm = _row_tile(t, tm)
    tn = _row_tile(n, tn)
    return pl.pallas_call(
        _mm_res_body,
        grid=(b, t // tm, n // tn),
        in_specs=[
            pl.BlockSpec((1, tm, k), lambda bi, i, j: (bi, i, 0)),
            pl.BlockSpec((k, tn), lambda bi, i, j: (0, j)),
            pl.BlockSpec((1, tm, tn), lambda bi, i, j: (bi, i, j)),
            pl.BlockSpec((1, 1, tn), lambda bi, i, j: (bi, 0, j)),
        ],
        out_specs=pl.BlockSpec((1, tm, tn), lambda bi, i, j: (bi, i, j)),
        out_shape=jax.ShapeDtypeStruct((b, t, n), F32),
        compiler_params=_cparams("parallel", "parallel", "parallel"),
        name="matmul_gated_residual",
    )(a, w, res, gate)


def _rmsnorm_body(x_ref, g_ref, o_ref):
    x = x_ref[0]
    ms = jnp.mean(x * x, axis=-1, keepdims=True)
    o_ref[0] = (x * lax.rsqrt(ms + EPS)) * g_ref[...]


def rmsnorm_final(x, g, *, tm=512):
    b, t, d = x.shape
    tm = _row_tile(t, tm)
    return pl.pallas_call(
        _rmsnorm_body,
        grid=(b, t // tm),
        in_specs=[pl.BlockSpec((1, tm, d), lambda bi, i: (bi, i, 0)),
                  pl.BlockSpec((1, d), lambda bi, i: (0, 0))],
        out_specs=pl.BlockSpec((1, tm, d), lambda bi, i: (bi, i, 0)),
        out_shape=jax.ShapeDtypeStruct((b, t, d), F32),
        compiler_params=_cparams("parallel", "parallel"),
        name="rmsnorm_final",
    )(x, g.reshape(1, d))


def _na_body(q_ref, k0_ref, k1_ref, k2_ref, v0_ref, v1_ref, v2_ref, kc_ref, vc_ref, bias_ref, o_ref,
             *, heads, hd, qb):
    dn = (((1,), (1,)), ((), ()))
    k_refs = (k0_ref, k1_ref, k2_ref)
    v_refs = (v0_ref, v1_ref, v2_ref)
    for g in range(heads):
        sl = slice(g * hd, (g + 1) * hd)
        q = q_ref[0, :, sl]
        s = [lax.dot_general(q, k_refs[i][0, :, sl], dn, preferred_element_type=F32)
             + bias_ref[0, g, :, i * qb:(i + 1) * qb] for i in range(3)]
        s.append(lax.dot_general(q, kc_ref[0, :, sl], dn, preferred_element_type=F32))
        m = functools.reduce(jnp.maximum, [jnp.max(si, axis=-1, keepdims=True) for si in s])
        p = [jnp.exp(si - m) for si in s]
        l = functools.reduce(jnp.add, [jnp.sum(pi, axis=-1, keepdims=True) for pi in p])
        vals = [v_refs[i][0, :, sl] for i in range(3)] + [vc_ref[0, :, sl]]
        o = functools.reduce(jnp.add, [jnp.dot(pi.astype(BF16), vi, preferred_element_type=F32)
                                       for pi, vi in zip(p, vals)])
        o_ref[0, :, sl] = (o / l).astype(o_ref.dtype)


def _na_bias_table(rpb, rows):
    r_blk = NA_ROWS_PER_BLOCK
    nblk = rows // r_blk
    kr = min(NA_KR_MAX, rows)
    tables = []
    for rb in (0, 1, nblk - 1):
        ks = min(max(rb - 1, 0), nblk - 3) * r_blk
        q_row = rb * r_blk + np.arange(r_blk)
        rs = np.clip(q_row - kr // 2, 0, rows - kr)
        k_row = ks + np.arange(3 * r_blk)
        ok_row = (k_row[None, :] >= rs[:, None]) & (k_row[None, :] < rs[:, None] + kr)
        rel_row = np.clip(k_row[None, :] - q_row[:, None] + (NA_KR_MAX - 1), 0, 2 * NA_KR_MAX - 2)
        q_col = np.arange(GRID_W)
        cs = np.clip(q_col - NA_KC // 2, 0, GRID_W - NA_KC)
        k_col = np.arange(GRID_W)
        ok_col = (k_col[None, :] >= cs[:, None]) & (k_col[None, :] < cs[:, None] + NA_KC)
        rel_col = np.clip(k_col[None, :] - q_col[:, None] + (NA_KC - 1), 0, 2 * NA_KC - 2)
        t = rpb[:, rel_row][:, :, :, rel_col]
        ok = ok_row[:, :, None, None] & ok_col[None, None, :, :]
        t = jnp.where(ok[None], t, MASK_VALUE)
        t = jnp.transpose(t, (0, 1, 3, 2, 4))
        tables.append(t.reshape(rpb.shape[0], r_blk * GRID_W, 3 * r_blk * GRID_W))
    return jnp.stack(tables)


def neighbourhood_attention(qkv, qkv_ctx, rpb):
    b, t, d3 = qkv.shape
    d = d3 // 3
    l = qkv_ctx.shape[1]
    hd = d // NA_HEADS
    rows = t // GRID_W
    r_blk = NA_ROWS_PER_BLOCK
    qb = r_blk * GRID_W
    nblk = rows // r_blk
    assert rows % r_blk == 0 and nblk >= 3 and min(NA_KR_MAX, rows) + r_blk <= 3 * r_blk
    g = NA_HEADS_PER_STEP
    gw = g * hd
    nhg = NA_HEADS // g
    bias = _na_bias_table(rpb.astype(F32), rows)

    def kstart(rb):
        return jnp.clip(rb - 1, 0, nblk - 3)

    def pattern(rb):
        return jnp.where(rb == 0, 0, jnp.where(rb == nblk - 1, 2, 1))

    q_spec = pl.BlockSpec((1, qb, gw), lambda hg, bi, rb: (bi, rb, hg))
    k_specs = [pl.BlockSpec((1, qb, gw), functools.partial(
        lambda hg, bi, rb, i: (bi, kstart(rb) + i, nhg + hg), i=i)) for i in range(3)]
    v_specs = [pl.BlockSpec((1, qb, gw), functools.partial(
        lambda hg, bi, rb, i: (bi, kstart(rb) + i, 2 * nhg + hg), i=i)) for i in range(3)]
    kc_spec = pl.BlockSpec((1, l, gw), lambda hg, bi, rb: (bi, 0, nhg + hg))
    vc_spec = pl.BlockSpec((1, l, gw), lambda hg, bi, rb: (bi, 0, 2 * nhg + hg))
    bias_spec = pl.BlockSpec((1, g, qb, 3 * qb), lambda hg, bi, rb: (pattern(rb), hg, 0, 0))
    return pl.pallas_call(
        functools.partial(_na_body, heads=g, hd=hd, qb=qb),
        grid=(nhg, b, nblk),
        in_specs=[q_spec] + k_specs + v_specs + [kc_spec, vc_spec, bias_spec],
        out_specs=pl.BlockSpec((1, qb, gw), lambda hg, bi, rb: (bi, rb, hg)),
        out_shape=jax.ShapeDtypeStruct((b, t, d), BF16),
        compiler_params=_cparams("parallel", "parallel", "arbitrary"),
        name="neighbourhood_attention",
    )(qkv, qkv, qkv, qkv, qkv, qkv, qkv, qkv_ctx, qkv_ctx, bias)


MOE_GATHER_UNROLL = 8


def _moe_ffn_body(rows_ref, h_hbm, w1_ref, w3_ref, w2_ref, o_ref, xbuf, sem):
    i = pl.program_id(1)
    n_tiles = pl.num_programs(1)
    tm = xbuf.shape[1]
    slot = lax.rem(i, 2)

    def issue(tile, sl):
        def body(r, carry):
            src = rows_ref[0, 0, tile * tm + r]
            pltpu.make_async_copy(h_hbm.at[pl.ds(src, 1)], xbuf.at[sl, pl.ds(r, 1)], sem.at[sl]).start()
            return carry
        lax.fori_loop(0, tm, body, 0, unroll=MOE_GATHER_UNROLL)

    @pl.when(i == 0)
    def _():
        issue(i, slot)

    pltpu.make_async_copy(h_hbm.at[pl.ds(0, tm)], xbuf.at[slot], sem.at[slot]).wait()

    @pl.when(i + 1 < n_tiles)
    def _():
        issue(i + 1, 1 - slot)

    x = xbuf[slot].astype(BF16)
    a = jnp.dot(x, w1_ref[0], preferred_element_type=F32)
    bb = jnp.dot(x, w3_ref[0], preferred_element_type=F32)
    hm = (a * jax.nn.sigmoid(a) * bb).astype(BF16)
    o_ref[0] = jnp.dot(hm, w2_ref[0], preferred_element_type=F32).astype(o_ref.dtype)


def moe_ffn(h, rows, w1, w3, w2, *, tm=256):
    e, m = rows.shape
    d = h.shape[1]
    f = w1.shape[2]
    tm = _row_tile(m, tm)
    return pl.pallas_call(
        _moe_ffn_body,
        grid=(e, m // tm),
        in_specs=[
            pl.BlockSpec((1, 1, m), lambda ei, i: (ei, 0, 0), memory_space=pltpu.SMEM),
            pl.BlockSpec(memory_space=pl.ANY),
            pl.BlockSpec((1, d, f), lambda ei, i: (ei, 0, 0)),
            pl.BlockSpec((1, d, f), lambda ei, i: (ei, 0, 0)),
            pl.BlockSpec((1, f, d), lambda ei, i: (ei, 0, 0)),
        ],
        out_specs=pl.BlockSpec((1, tm, d), lambda ei, i: (ei, i, 0)),
        out_shape=jax.ShapeDtypeStruct((e, m, d), BF16),
        scratch_shapes=[pltpu.VMEM((2, tm, d), F32), pltpu.SemaphoreType.DMA((2,))],
        compiler_params=_cparams("parallel", "arbitrary"),
        name="moe_ffn",
    )(rows.reshape(e, 1, m), h, w1, w3, w2)


COMBINE_TOKENS = 256
COMBINE_WINDOW = 64
BF16_SUBLANES = 16


def _combine_body(ts_ref, pos_ref, gs_ref, x_ref, g2_ref, ye_hbm, o_ref, ybuf, yextra, sem, sem_x,
                  *, n_exp, cap, win, n_batch):
    bi = pl.program_id(0)
    ti = pl.program_id(1)
    nt = pl.num_programs(1)
    step = bi * nt + ti
    total = n_batch * nt
    slot = lax.rem(step, 2)
    tt = pos_ref.shape[2]

    def tile_start(b, t, e):
        return ts_ref[(b * (nt + 1) + t) * n_exp + e]

    def window_start(b, t, e, w):
        st = tile_start(b, t, e)
        a = lax.shift_left(lax.shift_right_logical(st, 4), 4) + w * win
        return jnp.minimum(a, cap - win)

    def window_copy(b, e, a, dst, dsem):
        row0 = pl.multiple_of((e * n_batch + b) * cap + a, BF16_SUBLANES)
        return pltpu.make_async_copy(ye_hbm.at[pl.ds(row0, win)], dst, dsem)

    def issue(b, t, sl):
        for e in range(n_exp):
            window_copy(b, e, window_start(b, t, e, 0), ybuf.at[sl, e], sem.at[sl]).start()

    @pl.when(step == 0)
    def _():
        issue(bi, ti, slot)

    for e in range(n_exp):
        window_copy(bi, e, 0, ybuf.at[slot, e], sem.at[slot]).wait()

    @pl.when(step + 1 < total)
    def _():
        nxt = step + 1
        issue(nxt // nt, lax.rem(nxt, nt), 1 - slot)

    m_iota = lax.broadcasted_iota(jnp.int32, (win, tt), 0)
    dn = (((0,), (0,)), ((), ()))

    def gather_matrix(w):
        rows = []
        for e in range(n_exp):
            a = window_start(bi, ti, e, w)
            lo = window_start(bi, ti, e, 0) + w * win
            p = pos_ref[0, e:e + 1, :]
            hit = (p - a == m_iota) & (p >= lo)
            rows.append(jnp.where(hit, gs_ref[0, e:e + 1, :], 0.0).astype(BF16))
        return jnp.concatenate(rows, axis=0)

    d = ybuf.shape[-1]
    acc = lax.dot_general(gather_matrix(0), ybuf[slot].reshape(n_exp * win, d), dn,
                          preferred_element_type=F32)

    n_win = jnp.int32(1)
    for e in range(n_exp):
        span = tile_start(bi, ti + 1, e) - window_start(bi, ti, e, 0)
        n_win = jnp.maximum(n_win, (span + win - 1) // win)

    def extra(w, acc):
        for e in range(n_exp):
            cp = window_copy(bi, e, window_start(bi, ti, e, w), yextra.at[e], sem_x)
            cp.start()
            cp.wait()
        return acc + lax.dot_general(gather_matrix(w), yextra[...].reshape(n_exp * win, d), dn,
                                     preferred_element_type=F32)

    acc = lax.fori_loop(1, n_win, extra, acc)
    o_ref[0] = x_ref[0] + g2_ref[0] * acc


def moe_combine(x, g2, ye, pos, gs, tile_start, cap):
    b, t, d = x.shape
    e = pos.shape[1]
    tt = min(COMBINE_TOKENS, t)
    win = min(COMBINE_WINDOW, cap)
    nt = t // tt
    grid_spec = pltpu.PrefetchScalarGridSpec(
        num_scalar_prefetch=1,
        grid=(b, nt),
        in_specs=[
            pl.BlockSpec((1, e, tt), lambda bi, ti, ts: (bi, 0, ti)),
            pl.BlockSpec((1, e, tt), lambda bi, ti, ts: (bi, 0, ti)),
            pl.BlockSpec((1, tt, d), lambda bi, ti, ts: (bi, ti, 0)),
            pl.BlockSpec((1, 1, d), lambda bi, ti, ts: (bi, 0, 0)),
            pl.BlockSpec(memory_space=pl.ANY),
        ],
        out_specs=pl.BlockSpec((1, tt, d), lambda bi, ti, ts: (bi, ti, 0)),
        scratch_shapes=[
            pltpu.VMEM((2, e, win, d), BF16),
            pltpu.VMEM((e, win, d), BF16),
            pltpu.SemaphoreType.DMA((2,)),
            pltpu.SemaphoreType.DMA(()),
        ],
    )
    return pl.pallas_call(
        functools.partial(_combine_body, n_exp=e, cap=cap, win=win, n_batch=b),
        grid_spec=grid_spec,
        out_shape=jax.ShapeDtypeStruct((b, t, d), F32),
        compiler_params=_cparams("arbitrary", "arbitrary"),
        name="moe_combine",
    )(tile_start, pos, gs, x, g2, ye)


def expert_choice_moe(x, g, shift, scale, g2, router_w, w1, w3, w2):
    b, t, d = x.shape
    e = router_w.shape[1]
    cap = EC_CAPACITY_FACTOR * t // e
    h, logits = norm_mod_router(x, g, shift, scale, router_w)
    pos, gs, idx, rowstart = route(logits, cap)
    rows = idx + (jnp.arange(b, dtype=jnp.int32) * t)[:, None, None]
    rows = jnp.swapaxes(rows, 0, 1).reshape(e, b * cap)
    ye = moe_ffn(h.reshape(b * t, d), rows, w1, w3, w2).reshape(e * b * cap, d)
    tt = min(COMBINE_TOKENS, t)
    ts = rowstart[:, :, ::tt // LANES][:, :, :t // tt]
    ts = jnp.concatenate([jnp.swapaxes(ts, 1, 2), jnp.full((b, 1, e), cap, jnp.int32)], axis=1)
    return moe_combine(x, g2, ye, pos, gs, ts.reshape(-1), cap)


def _axial_rope_angles(n_tok, head_dim):
    n_freq = head_dim // 4
    inv = ROPE_BASE ** (-jnp.arange(n_freq, dtype=F32) / n_freq)
    tt = jnp.arange(n_tok)
    row = (tt // GRID_W).astype(F32)
    col = (tt % GRID_W).astype(F32)
    return row[:, None] * inv[None], col[:, None] * inv[None]


GLA_BLOCK = 4 * GLA_CHUNK


def _log_sigmoid(x):
    return jnp.minimum(x, 0.0) - jnp.log(1.0 + jnp.exp(-jnp.abs(x)))


def _gla_block(q_ref, k_ref, v_ref, lr_ref, cos_ref, sin_ref, a_ref, ab_ref, tri_ref, perm_ref, st_ref, o_ref,
               *, reverse):
    nt = (((1,), (1,)), ((), ()))
    tn = (((0,), (0,)), ((), ()))
    hi = lax.Precision.HIGHEST
    n = GLA_BLOCK
    c = GLA_CHUNK
    pre = jnp.dot(lr_ref[0], a_ref[0], preferred_element_type=F32, precision=hi) + ab_ref[0]
    la = _log_sigmoid(pre) / GLA_GATE_NORM
    la_hi = la.astype(BF16)
    la_r = la - la_hi.astype(F32)
    la_mid = la_r.astype(BF16)
    la_lo = (la_r - la_mid.astype(F32)).astype(BF16)
    tri = tri_ref[...]
    bcum = (jnp.dot(tri, la_hi, preferred_element_type=F32) + jnp.dot(tri, la_mid, preferred_element_type=F32)
            + jnp.dot(tri, la_lo, preferred_element_type=F32))
    q = q_ref[0]
    k = k_ref[0]
    v = v_ref[0]
    qf = q.astype(F32)
    kf = k.astype(F32)
    if cos_ref is not None:
        cs, sn = cos_ref[...], sin_ref[...]
        qf = qf * cs + jnp.dot(q, perm_ref[...], preferred_element_type=F32) * sn
        kf = kf * cs + jnp.dot(k, perm_ref[...], preferred_element_type=F32) * sn
    last_row = [bcum[(i * c if reverse else i * c + c - 1):(i * c + 1 if reverse else i * c + c)]
                for i in range(n // c)]
    blast = jnp.concatenate([jnp.broadcast_to(r, (c, r.shape[1])) for r in last_row], axis=0)
    qe = (qf * jnp.exp(bcum)).astype(BF16)
    ke = (kf * jnp.exp(-bcum)).astype(BF16)
    kd = (kf * jnp.exp(blast - bcum)).astype(BF16)
    row = lax.broadcasted_iota(jnp.int32, (n, n), 0)
    col = lax.broadcasted_iota(jnp.int32, (n, n), 1)
    same_chunk = (row // c) == (col // c)
    causal = (col >= row) if reverse else (col <= row)
    att = jnp.where(same_chunk & causal, lax.dot_general(qe, ke, nt, preferred_element_type=F32), 0.0)
    o_intra = jnp.dot(att.astype(BF16), v, preferred_element_type=F32)
    order = range(n // c - 1, -1, -1) if reverse else range(n // c)
    for i in order:
        sl = slice(i * c, (i + 1) * c)
        st = st_ref[...]
        o_ref[0, 0, sl, :] = o_intra[sl] + lax.dot_general(qe[sl], st.astype(BF16), nt,
                                                           preferred_element_type=F32)
        u_t = lax.dot_general(v[sl], kd[sl], tn, preferred_element_type=F32)
        st_ref[...] = st * jnp.exp(last_row[i]) + u_t


def _gla_body(qc_ref, kc_ref, vc_ref, lrc_ref, ql_ref, kl_ref, vl_ref, lrl_ref, cos_ref, sin_ref,
              a_ref, ab_ref, tri_ref, perm_ref, oc_ref, ol_ref, st_ref, *, reverse):
    j = pl.program_id(2)

    @pl.when(j == 0)
    def _():
        st_ref[...] = jnp.zeros_like(st_ref)
        _gla_block(qc_ref, kc_ref, vc_ref, lrc_ref, None, None, a_ref, ab_ref, tri_ref, perm_ref, st_ref,
                   oc_ref, reverse=reverse)

    @pl.when(j > 0)
    def _():
        _gla_block(ql_ref, kl_ref, vl_ref, lrl_ref, cos_ref, sin_ref, a_ref, ab_ref, tri_ref, perm_ref,
                   st_ref, ol_ref, reverse=reverse)


def gla_scan(p_c, lr_c, p_l, lr_l, a_up, a_b, cos, sin, *, dk, dv, reverse):
    b, tc, _ = p_c.shape
    tl = p_l.shape[1]
    hk, hv = dk // GLA_HEADS, dv // GLA_HEADS
    n = GLA_BLOCK
    assert tc == n and tl % n == 0 and hk == LANES
    nl = tl // n
    qo, ko, vo = 0, dk // hk, 2 * dk // hv
    ii = np.arange(n)
    same = ii[:, None] // GLA_CHUNK == ii[None, :] // GLA_CHUNK
    tri = jnp.asarray(same & ((ii[None, :] >= ii[:, None]) if reverse else (ii[None, :] <= ii[:, None])), BF16)
    jj = np.arange(hk)
    partner = np.where((jj % (hk // 2)) < hk // 4, jj + hk // 4, jj - hk // 4)
    perm = np.zeros((hk, hk), np.float32)
    perm[partner, jj] = 1.0
    perm = jnp.asarray(perm, BF16)

    def lat_blk(j):
        blk = jnp.clip(j - 1, 0, nl - 1)
        return (nl - 1 - blk) if reverse else blk

    ctx_spec = lambda width, off: pl.BlockSpec((1, n, width), lambda bi, h, j: (bi, 0, off + h))
    lat_spec = lambda width, off: pl.BlockSpec((1, n, width), lambda bi, h, j: (bi, lat_blk(j), off + h))
    in_specs = [
        ctx_spec(hk, qo), ctx_spec(hk, ko), ctx_spec(hv, vo),
        pl.BlockSpec((1, n, LANES), lambda bi, h, j: (bi, 0, 0)),
        lat_spec(hk, qo), lat_spec(hk, ko), lat_spec(hv, vo),
        pl.BlockSpec((1, n, LANES), lambda bi, h, j: (bi, lat_blk(j), 0)),
        pl.BlockSpec((n, hk), lambda bi, h, j: (lat_blk(j), 0)),
        pl.BlockSpec((n, hk), lambda bi, h, j: (lat_blk(j), 0)),
        pl.BlockSpec((1, LANES, hk), lambda bi, h, j: (0, 0, h)),
        pl.BlockSpec((1, 1, hk), lambda bi, h, j: (0, 0, h)),
        pl.BlockSpec((n, n), lambda bi, h, j: (0, 0)),
        pl.BlockSpec((hk, hk), lambda bi, h, j: (0, 0)),
    ]
    out_specs = (
        pl.BlockSpec((1, 1, n, hv), lambda bi, h, j: (0, bi, 0, h)),
        pl.BlockSpec((1, 1, n, hv), lambda bi, h, j: (0, bi, lat_blk(j), h)),
    )
    o_c, o_l = pl.pallas_call(
        functools.partial(_gla_body, reverse=reverse),
        grid=(b, GLA_HEADS, nl + 1),
        in_specs=in_specs,
        out_specs=out_specs,
        out_shape=(jax.ShapeDtypeStruct((1, b, tc, dv), F32), jax.ShapeDtypeStruct((1, b, tl, dv), F32)),
        scratch_shapes=[pltpu.VMEM((hv, hk), F32)],
        compiler_params=_cparams("parallel", "parallel", "arbitrary"),
        name="gla_scan_bwd" if reverse else "gla_scan_fwd",
    )(p_c, p_c, p_c, lr_c, p_l, p_l, p_l, lr_l, cos, sin, a_up[None], a_b[None], tri, perm)
    return o_c[0], o_l[0]


def _gla_out_body(of_ref, ob_ref, g_ref, ng_ref, y_ref, *, heads):
    o = of_ref[0] + ob_ref[0]
    hv = o.shape[1] // heads
    outs = []
    for h in range(heads):
        oh = o[:, h * hv:(h + 1) * hv]
        outs.append(oh * lax.rsqrt(jnp.mean(oh * oh, axis=-1, keepdims=True) + EPS))
    on = jnp.concatenate(outs, axis=1) * ng_ref[...]
    gg = g_ref[0].astype(F32)
    y_ref[0] = (on * (gg * jax.nn.sigmoid(gg))).astype(y_ref.dtype)


def gla_output(o_f, o_b, p, norm_g, *, g_off, tm=256):
    b, t, dv = o_f.shape
    tm = _row_tile(t, tm)
    blk = pl.BlockSpec((1, tm, dv), lambda bi, i: (bi, i, 0))
    return pl.pallas_call(
        functools.partial(_gla_out_body, heads=GLA_HEADS),
        grid=(b, t // tm),
        in_specs=[blk, blk, pl.BlockSpec((1, tm, dv), lambda bi, i: (bi, i, g_off // dv)),
                  pl.BlockSpec((1, dv), lambda bi, i: (0, 0))],
        out_specs=blk,
        out_shape=jax.ShapeDtypeStruct((b, t, dv), BF16),
        compiler_params=_cparams("parallel", "parallel"),
        name="gla_output",
    )(o_f, o_b, p, norm_g.reshape(1, dv))


FFT_T2 = 128
FFT_ROWS = 8


def _dft_mats(n):
    ang = 2.0 * np.pi * ((np.arange(n)[:, None] * np.arange(n)[None, :]) % n) / n
    return np.cos(ang), np.sin(ang)


def _chan_dft_body(u_ref, w_ref, z_ref):
    z_ref[0] = jnp.dot(u_ref[0], w_ref[...], preferred_element_type=F32)


def channel_dft(p, *, u_off, fd, tm=512):
    b, t, _ = p.shape
    gd = fd // FNET_GROUPS
    c, s = _dft_mats(gd)
    w = jnp.asarray(np.concatenate([c, -s], axis=1), BF16)
    tm = _row_tile(t, tm)
    return pl.pallas_call(
        _chan_dft_body,
        grid=(b, t // tm, FNET_GROUPS),
        in_specs=[pl.BlockSpec((1, tm, gd), lambda bi, i, g: (bi, i, u_off // gd + g)),
                  pl.BlockSpec((gd, 2 * gd), lambda bi, i, g: (0, 0))],
        out_specs=pl.BlockSpec((1, tm, 2 * gd), lambda bi, i, g: (bi, i, g)),
        out_shape=jax.ShapeDtypeStruct((b, t, 2 * fd), F32),
        compiler_params=_cparams("parallel", "parallel", "parallel"),
        name="channel_dft",
    )(p, w)


def _dft_direct_body(z_ref, f_ref, y_ref, *, gd, scale):
    z = z_ref[0]
    zz = jnp.concatenate([z[:, :gd], z[:, gd:]], axis=0).astype(BF16)
    y_ref[0] = jnp.dot(f_ref[...], zz, preferred_element_type=F32) * scale


def token_dft_direct(z, gd):
    b, t, w2 = z.shape
    c, s = _dft_mats(t)
    f = jnp.asarray(np.concatenate([c, s], axis=1), BF16)
    return pl.pallas_call(
        functools.partial(_dft_direct_body, gd=gd, scale=float((t * gd) ** -0.5)),
        grid=(b, FNET_GROUPS),
        in_specs=[pl.BlockSpec((1, t, 2 * gd), lambda bi, g: (bi, 0, g)),
                  pl.BlockSpec((t, 2 * t), lambda bi, g: (0, 0))],
        out_specs=pl.BlockSpec((1, t, gd), lambda bi, g: (bi, 0, g)),
        out_shape=jax.ShapeDtypeStruct((b, t, w2 // 2), F32),
        compiler_params=_cparams("parallel", "parallel"),
        name="token_dft_direct",
    )(z, f)


def _dft_stage1_body(z_ref, f_ref, twc_ref, tws_ref, o_ref, *, gd):
    for j in range(FFT_ROWS):
        z = z_ref[0, :, j, :]
        zz = jnp.concatenate([z[:, :gd], z[:, gd:]], axis=0).astype(BF16)
        a = jnp.dot(f_ref[...], zz, preferred_element_type=F32)
        t1 = a.shape[0] // 2
        ar, ai = a[:t1], a[t1:]
        tc = jnp.concatenate([twc_ref[j]] * (gd // LANES), axis=1)
        ts = jnp.concatenate([tws_ref[j]] * (gd // LANES), axis=1)
        o_ref[0, 0, :, j, :gd] = ar * tc + ai * ts
        o_ref[0, 0, :, j, gd:] = ai * tc - ar * ts


def _dft_stage2_body(b_ref, f_ref, y_ref, *, gd, scale):
    for j in range(FFT_ROWS):
        s = b_ref[0, 0, j]
        ss = jnp.concatenate([s[:, :gd], s[:, gd:]], axis=0).astype(BF16)
        y_ref[0, :, j, :] = jnp.dot(f_ref[...], ss, preferred_element_type=F32) * scale


def token_dft_two_stage(z, gd):
    b, t, w2 = z.shape
    t2 = FFT_T2
    t1 = t // t2
    assert t1 * t2 == t and t1 % FFT_ROWS == 0 and t2 % FFT_ROWS == 0 and gd % LANES == 0
    c1, s1 = _dft_mats(t1)
    f1 = jnp.asarray(np.block([[c1, s1], [-s1, c1]]), BF16)
    ang = 2.0 * np.pi * (np.arange(t2)[:, None] * np.arange(t1)[None, :]) / t
    twc = jnp.asarray(np.broadcast_to(np.cos(ang)[:, :, None], (t2, t1, LANES)), F32)
    tws = jnp.asarray(np.broadcast_to(np.sin(ang)[:, :, None], (t2, t1, LANES)), F32)
    c2, s2 = _dft_mats(t2)
    f2 = jnp.asarray(np.concatenate([c2, s2], axis=1), BF16)
    z4 = z.reshape(b, t1, t2, w2)
    bmid = pl.pallas_call(
        functools.partial(_dft_stage1_body, gd=gd),
        grid=(b, FNET_GROUPS, t2 // FFT_ROWS),
        in_specs=[pl.BlockSpec((1, t1, FFT_ROWS, 2 * gd), lambda bi, g, i: (bi, 0, i, g)),
                  pl.BlockSpec((2 * t1, 2 * t1), lambda bi, g, i: (0, 0)),
                  pl.BlockSpec((FFT_ROWS, t1, LANES), lambda bi, g, i: (i, 0, 0)),
                  pl.BlockSpec((FFT_ROWS, t1, LANES), lambda bi, g, i: (i, 0, 0))],
        out_specs=pl.BlockSpec((1, 1, t1, FFT_ROWS, 2 * gd), lambda bi, g, i: (bi, g, 0, i, 0)),
        out_shape=jax.ShapeDtypeStruct((b, FNET_GROUPS, t1, t2, 2 * gd), F32),
        compiler_params=_cparams("parallel", "parallel", "parallel"),
        name="token_dft_stage1",
    )(z4, f1, twc, tws)
    y = pl.pallas_call(
        functools.partial(_dft_stage2_body, gd=gd, scale=float((t * gd) ** -0.5)),
        grid=(b, FNET_GROUPS, t1 // FFT_ROWS),
        in_specs=[pl.BlockSpec((1, 1, FFT_ROWS, t2, 2 * gd), lambda bi, g, i: (bi, g, i, 0, 0)),
                  pl.BlockSpec((t2, 2 * t2), lambda bi, g, i: (0, 0))],
        out_specs=pl.BlockSpec((1, t2, FFT_ROWS, gd), lambda bi, g, i: (bi, 0, i, g)),
        out_shape=jax.ShapeDtypeStruct((b, t2, t1, w2 // 2), F32),
        compiler_params=_cparams("parallel", "parallel", "parallel"),
        name="token_dft_stage2",
    )(bmid, f2)
    return y.reshape(b, t, w2 // 2)


def fourier_mix(p, *, u_off, fd):
    gd = fd // FNET_GROUPS
    z = channel_dft(p, u_off=u_off, fd=fd)
    if z.shape[1] <= 512:
        return token_dft_direct(z, gd)
    return token_dft_two_stage(z, gd)


def _mm2_res_body(a1_ref, a2_ref, w1_ref, w2_ref, r_ref, g_ref, o_ref):
    acc = jnp.dot(a1_ref[0].astype(BF16), w1_ref[...], preferred_element_type=F32)
    acc = acc + jnp.dot(a2_ref[0].astype(BF16), w2_ref[...], preferred_element_type=F32)
    o_ref[0] = r_ref[0] + g_ref[0] * acc


def matmul2_gated_residual(a1, a2, w, res, gate, *, tm=512, tn=2048):
    b, t, k1 = a1.shape
    k2 = a2.shape[2]
    n = w.shape[1]
    tm = _row_tile(t, tm)
    tn = _row_tile(n, tn)
    return pl.pallas_call(
        _mm2_res_body,
        grid=(b, t // tm, n // tn),
        in_specs=[
            pl.BlockSpec((1, tm, k1), lambda bi, i, j: (bi, i, 0)),
            pl.BlockSpec((1, tm, k2), lambda bi, i, j: (bi, i, 0)),
            pl.BlockSpec((k1, tn), lambda bi, i, j: (0, j)),
            pl.BlockSpec((k2, tn), lambda bi, i, j: (k1 // k2, j)),
            pl.BlockSpec((1, tm, tn), lambda bi, i, j: (bi, i, j)),
            pl.BlockSpec((1, 1, tn), lambda bi, i, j: (bi, 0, j)),
        ],
        out_specs=pl.BlockSpec((1, tm, tn), lambda bi, i, j: (bi, i, j)),
        out_shape=jax.ShapeDtypeStruct((b, t, n), F32),
        compiler_params=_cparams("parallel", "parallel", "parallel"),
        name="matmul2_gated_residual",
    )(a1, a2, w, w, res, gate)


def mixer_gla_fnet(x_ctx, x_lat, mod_ctx, mod_lat, norm_g1, w_in, w_out,
                   a_up_f, a_b_f, a_up_b, a_b_b, gla_norm_g, last):
    dv = gla_norm_g.shape[0]
    dk = a_up_f.shape[1]
    t = x_lat.shape[1]
    hk = dk // GLA_HEADS
    n_main = 2 * dk + 2 * dv
    fd = w_in.shape[1] - n_main - 2 * GLA_RANK
    w_main = jnp.concatenate([w_in[:, :n_main], w_in[:, n_main + 2 * GLA_RANK:]], axis=1).astype(BF16)
    w_lr = jnp.pad(w_in[:, n_main:n_main + 2 * GLA_RANK], ((0, 0), (0, LANES - 2 * GLA_RANK)))
    cs = jnp.concatenate([jnp.full((dk,), hk ** -0.5, F32), jnp.ones((w_main.shape[1] - dk,), F32)])
    p_l, lr_l = norm_mod_matmul(x_lat, norm_g1, mod_lat[0], mod_lat[1], w_main, cs, w_aux=w_lr)
    p_c, lr_c = norm_mod_matmul(x_ctx, norm_g1, mod_ctx[0], mod_ctx[1], w_main, cs, w_aux=w_lr)
    ang_row, ang_col = _axial_rope_angles(t, hk)
    cos = jnp.concatenate([jnp.cos(ang_row)] * 2 + [jnp.cos(ang_col)] * 2, axis=1)
    sin = jnp.concatenate([-jnp.sin(ang_row), jnp.sin(ang_row), -jnp.sin(ang_col), jnp.sin(ang_col)], axis=1)
    pad_f = ((0, LANES - GLA_RANK), (0, 0))
    pad_b = ((GLA_RANK, LANES - 2 * GLA_RANK), (0, 0))
    ofc, ofl = gla_scan(p_c, lr_c, p_l, lr_l, jnp.pad(a_up_f, pad_f), a_b_f[None], cos, sin,
                        dk=dk, dv=dv, reverse=False)
    obc, obl = gla_scan(p_c, lr_c, p_l, lr_l, jnp.pad(a_up_b, pad_b), a_b_b[None], cos, sin,
                        dk=dk, dv=dv, reverse=True)
    w_out_b = w_out.astype(BF16)
    g_off, u_off = 2 * dk + dv, 2 * dk + 2 * dv
    x_lat = matmul2_gated_residual(gla_output(ofl, obl, p_l, gla_norm_g, g_off=g_off),
                                   fourier_mix(p_l, u_off=u_off, fd=fd), w_out_b, x_lat, mod_lat[2])
    if not last:
        x_ctx = matmul2_gated_residual(gla_output(ofc, obc, p_c, gla_norm_g, g_off=g_off),
                                       fourier_mix(p_c, u_off=u_off, fd=fd), w_out_b, x_ctx, mod_ctx[2])
    return x_ctx, x_lat


def _ctx_attn_body(q_ref, k_ref, v_ref, o_ref, *, heads, hd):
    nt = (((1,), (1,)), ((), ()))
    for h in range(heads):
        sl = slice(h * hd, (h + 1) * hd)
        s = lax.dot_general(q_ref[0, :, sl], k_ref[0, :, sl], nt, preferred_element_type=F32)
        p = jnp.exp(s - jnp.max(s, axis=-1, keepdims=True))
        o = jnp.dot(p.astype(BF16), v_ref[0, :, sl], preferred_element_type=F32)
        o_ref[0, :, sl] = (o / jnp.sum(p, axis=-1, keepdims=True)).astype(o_ref.dtype)


def context_self_attention(qkv_ctx):
    b, l, d3 = qkv_ctx.shape
    d = d3 // 3
    hd = d // NA_HEADS
    g = NA_HEADS_PER_STEP
    gw = g * hd
    nhg = NA_HEADS // g
    return pl.pallas_call(
        functools.partial(_ctx_attn_body, heads=g, hd=hd),
        grid=(b, nhg),
        in_specs=[pl.BlockSpec((1, l, gw), lambda bi, hg: (bi, 0, hg)),
                  pl.BlockSpec((1, l, gw), lambda bi, hg: (bi, 0, nhg + hg)),
                  pl.BlockSpec((1, l, gw), lambda bi, hg: (bi, 0, 2 * nhg + hg))],
        out_specs=pl.BlockSpec((1, l, gw), lambda bi, hg: (bi, 0, hg)),
        out_shape=jax.ShapeDtypeStruct((b, l, d), BF16),
        compiler_params=_cparams("parallel", "parallel"),
        name="context_self_attention",
    )(qkv_ctx, qkv_ctx, qkv_ctx)


def mixer_neighbourhood(x_ctx, x_lat, mod_ctx, mod_lat, norm_g1, w_qkv, w_out, rpb, last):
    bsz, t, d = x_lat.shape
    l = x_ctx.shape[1]
    hd = d // NA_HEADS
    w_qkv_b = w_qkv.astype(BF16)
    w_out_b = w_out.astype(BF16)
    cs = jnp.concatenate([jnp.full((d,), hd ** -0.5, F32), jnp.ones((2 * d,), F32)])
    qkv_l = norm_mod_matmul(x_lat, norm_g1, mod_lat[0], mod_lat[1], w_qkv_b, cs)
    qkv_c = norm_mod_matmul(x_ctx, norm_g1, mod_ctx[0], mod_ctx[1], w_qkv_b, cs)
    o_l = neighbourhood_attention(qkv_l, qkv_c, rpb)
    x_lat = matmul_gated_residual(o_l, w_out_b, x_lat, mod_lat[2])
    if not last:
        x_ctx = matmul_gated_residual(context_self_attention(qkv_c), w_out_b, x_ctx, mod_ctx[2])
    return x_ctx, x_lat


def kernel(x, c, ctx, c_ctx, ada_w, ada_b, norm1_g, norm2_g, ab_w_in, ab_w_out, gla_a_up_f, gla_a_b_f,
           gla_a_up_b, gla_a_b_b, gla_norm_g, na_w_qkv, na_w_out, na_rpb, router_w, exp_w1, exp_w3,
           exp_w2, final_g):
    depth = ada_w.shape[0]
    bsz, t, d = x.shape
    cond = jnp.concatenate([c, c_ctx[None], jnp.zeros((8 - bsz - 1, d), F32)], axis=0)
    mods = ada_all_layers(cond, ada_w, ada_b)
    xl, xc = x, ctx
    for l in range(depth):
        last = l == depth - 1
        m = mods[l].reshape(8, N_MOD, d)
        mod_lat = [m[:bsz, i][:, None, :] for i in range(N_MOD)]
        mod_ctx = [jnp.broadcast_to(m[bsz, i][None, None, :], (bsz, 1, d)) for i in range(N_MOD)]
        if l % 2 == 0:
            e = l // 2
            xc, xl = mixer_gla_fnet(xc, xl, mod_ctx[:3], mod_lat[:3], norm1_g[l], ab_w_in[e], ab_w_out[e],
                                    gla_a_up_f[e], gla_a_b_f[e], gla_a_up_b[e], gla_a_b_b[e],
                                    gla_norm_g[e], last)
        else:
            o = l // 2
            xc, xl = mixer_neighbourhood(xc, xl, mod_ctx[:3], mod_lat[:3], norm1_g[l], na_w_qkv[o],
                                         na_w_out[o], na_rpb[o], last)
        w1, w3, w2 = exp_w1[l].astype(BF16), exp_w3[l].astype(BF16), exp_w2[l].astype(BF16)
        xl = expert_choice_moe(xl, norm2_g[l], mod_lat[3], mod_lat[4], mod_lat[5], router_w[l], w1, w3, w2)
        if not last:
            xc = expert_choice_moe(xc, norm2_g[l], mod_ctx[3], mod_ctx[4], mod_ctx[5], router_w[l], w1, w3, w2)
    return rmsnorm_final(xl, final_g)
```

```python
import functools

import jax
import jax.numpy as jnp
import numpy as np
from jax import lax
from jax.experimental import pallas as pl
from jax.experimental.pallas import tpu as pltpu

F32 = jnp.float32
BF16 = jnp.bfloat16

GRID_W = 64
N_MOD = 6
EPS = 1e-6
GLA_HEADS = 4
GLA_RANK = 16
GLA_GATE_NORM = 16.0
GLA_CHUNK = 64
ROPE_BASE = 10000.0
FNET_GROUPS = 4
NA_HEADS = 16
NA_KR_MAX = 8
NA_KC = 16
N_EXPERTS = 16
EC_CAPACITY_FACTOR = 2

LANES = 128
VMEM_LIMIT_BYTES = 56 * 1024 * 1024
MASK_VALUE = -1e30

NA_ROWS_PER_BLOCK = 4
NA_HEADS_PER_STEP = 4


def _cparams(*sem):
    return pltpu.CompilerParams(dimension_semantics=sem, vmem_limit_bytes=VMEM_LIMIT_BYTES)


def _row_tile(t, cap):
    if t <= cap:
        return t
    for step in (LANES, 8):
        for tm in range(cap - cap % step, 0, -step):
            if t % tm == 0:
                return tm
    raise ValueError(f"no tile for {t} under {cap}")


def _ada_body(c_ref, w_ref, b_ref, o_ref):
    c = c_ref[...]
    a = (c * jax.nn.sigmoid(c)).astype(BF16)
    acc = jnp.dot(a, w_ref[0].astype(BF16), preferred_element_type=F32)
    o_ref[0] = acc + b_ref[0]


def ada_all_layers(cond, ada_w, ada_b):
    depth, d, n = ada_w.shape
    r = cond.shape[0]
    tn = _row_tile(n, 1024)
    return pl.pallas_call(
        _ada_body,
        grid=(depth, n // tn),
        in_specs=[
            pl.BlockSpec((r, d), lambda l, j: (0, 0)),
            pl.BlockSpec((1, d, tn), lambda l, j: (l, 0, j)),
            pl.BlockSpec((1, 1, tn), lambda l, j: (l, 0, j)),
        ],
        out_specs=pl.BlockSpec((1, r, tn), lambda l, j: (l, 0, j)),
        out_shape=jax.ShapeDtypeStruct((depth, r, n), F32),
        compiler_params=_cparams("parallel", "parallel"),
        name="ada_params",
    )(cond, ada_w, ada_b.reshape(depth, 1, n))


def _norm_mod(x, g, shift, scale):
    ms = jnp.mean(x * x, axis=-1, keepdims=True)
    y = x * lax.rsqrt(ms + EPS)
    return (y * g) * (1.0 + scale) + shift


def _nm_mm_body(x_ref, g_ref, sh_ref, sc_ref, w_ref, cs_ref, o_ref, h_scr):
    @pl.when(pl.program_id(2) == 0)
    def _():
        h = _norm_mod(x_ref[0], g_ref[...], sh_ref[0], sc_ref[0])
        h_scr[...] = h.astype(h_scr.dtype)

    acc = jnp.dot(h_scr[...], w_ref[...], preferred_element_type=F32)
    o_ref[0] = (acc * cs_ref[...]).astype(o_ref.dtype)


def _nm_mm_aux_body(x_ref, g_ref, sh_ref, sc_ref, w_ref, cs_ref, wa_ref, o_ref, oa_ref, h_scr):
    @pl.when(pl.program_id(2) == 0)
    def _():
        h = _norm_mod(x_ref[0], g_ref[...], sh_ref[0], sc_ref[0])
        h_scr[...] = h.astype(h_scr.dtype)
        oa_ref[0] = jnp.dot(h_scr[...], wa_ref[...], preferred_element_type=F32)

    acc = jnp.dot(h_scr[...], w_ref[...], preferred_element_type=F32)
    o_ref[0] = (acc * cs_ref[...]).astype(o_ref.dtype)


def norm_mod_matmul(x, g, shift, scale, w, col_scale, *, out_dtype=BF16, w_aux=None, tm=512, tn=2048):
    b, t, d = x.shape
    n = w.shape[1]
    tm = _row_tile(t, tm)
    tn = _row_tile(n, tn)
    in_specs = [
        pl.BlockSpec((1, tm, d), lambda bi, i, j: (bi, i, 0)),
        pl.BlockSpec((1, d), lambda bi, i, j: (0, 0)),
        pl.BlockSpec((1, 1, d), lambda bi, i, j: (bi, 0, 0)),
        pl.BlockSpec((1, 1, d), lambda bi, i, j: (bi, 0, 0)),
        pl.BlockSpec((d, tn), lambda bi, i, j: (0, j)),
        pl.BlockSpec((1, tn), lambda bi, i, j: (0, j)),
    ]
    args = [x, g.reshape(1, d), shift, scale, w, col_scale.reshape(1, n)]
    out_specs = pl.BlockSpec((1, tm, tn), lambda bi, i, j: (bi, i, j))
    out_shape = jax.ShapeDtypeStruct((b, t, n), out_dtype)
    body = _nm_mm_body
    if w_aux is not None:
        na = w_aux.shape[1]
        in_specs.append(pl.BlockSpec((d, na), lambda bi, i, j: (0, 0)))
        args.append(w_aux)
        out_specs = (out_specs, pl.BlockSpec((1, tm, na), lambda bi, i, j: (bi, i, 0)))
        out_shape = (out_shape, jax.ShapeDtypeStruct((b, t, na), F32))
        body = _nm_mm_aux_body
    return pl.pallas_call(
        body,
        grid=(b, t // tm, n // tn),
        in_specs=in_specs,
        out_specs=out_specs,
        out_shape=out_shape,
        scratch_shapes=[pltpu.VMEM((tm, d), BF16)],
        compiler_params=_cparams("parallel", "parallel", "arbitrary"),
        name="norm_mod_matmul",
    )(*args)


def _nm_router_body(x_ref, g_ref, sh_ref, sc_ref, rwt_ref, h_ref, lg_ref):
    h = _norm_mod(x_ref[0], g_ref[...], sh_ref[0], sc_ref[0])
    h_ref[0] = h.astype(h_ref.dtype)
    lg_ref[0] = lax.dot_general(rwt_ref[...], h, (((1,), (1,)), ((), ())),
                                preferred_element_type=F32, precision=lax.Precision.HIGHEST)


def norm_mod_router(x, g, shift, scale, router_w, *, tm=512):
    b, t, d = x.shape
    e = router_w.shape[1]
    tm = _row_tile(t, tm)
    return pl.pallas_call(
        _nm_router_body,
        grid=(b, t // tm),
        in_specs=[
            pl.BlockSpec((1, tm, d), lambda bi, i: (bi, i, 0)),
            pl.BlockSpec((1, d), lambda bi, i: (0, 0)),
            pl.BlockSpec((1, 1, d), lambda bi, i: (bi, 0, 0)),
            pl.BlockSpec((1, 1, d), lambda bi, i: (bi, 0, 0)),
            pl.BlockSpec((e, d), lambda bi, i: (0, 0)),
        ],
        out_specs=(
            pl.BlockSpec((1, tm, d), lambda bi, i: (bi, i, 0)),
            pl.BlockSpec((1, e, tm), lambda bi, i: (bi, 0, i)),
        ),
        out_shape=(jax.ShapeDtypeStruct((b, t, d), F32), jax.ShapeDtypeStruct((b, e, t), F32)),
        compiler_params=_cparams("parallel", "parallel"),
        name="norm_mod_router",
    )(x, g.reshape(1, d), shift, scale, router_w.T)


def _route_body(lg_ref, u_ref, lb_ref, ui_ref, pos_ref, gs_ref, idx_ref, rs_ref, *, cap, t_valid):
    _, e, r, _ = lg_ref.shape
    lg = lg_ref[0]
    mx = jnp.max(lg, axis=0, keepdims=True)
    ex = jnp.exp(lg - mx)
    aff = ex / jnp.sum(ex, axis=0, keepdims=True)
    if t_valid < r * LANES:
        tok = (lax.broadcasted_iota(jnp.int32, (e, r, LANES), 1) * LANES
               + lax.broadcasted_iota(jnp.int32, (e, r, LANES), 2))
        aff = jnp.where(tok < t_valid, aff, -1.0)
    bits = lax.bitcast_convert_type(aff, jnp.int32)

    def count(mask):
        s1 = jnp.sum(jnp.where(mask, 1.0, 0.0), axis=1, keepdims=True)
        return jnp.sum(s1, axis=2, keepdims=True)

    def search(i, prefix):
        cand = prefix | lax.shift_left(jnp.int32(1), 30 - i)
        return jnp.where(count(bits >= cand) >= cap, cand, prefix)

    thr = lax.fori_loop(0, 31, search, jnp.zeros((e, 1, 1), jnp.int32))

    def cumsum_incl(mask):
        x2 = jnp.where(mask, 1.0, 0.0).reshape(e * r, LANES).astype(BF16)
        within = jnp.dot(x2, u_ref[...], preferred_element_type=F32)
        rowtot = jnp.broadcast_to(within[:, LANES - 1:LANES], (e * r, LANES)).astype(BF16)
        rowoff = jnp.dot(lb_ref[...], rowtot, preferred_element_type=F32)
        return within.reshape(e, r, LANES), rowoff.reshape(e, r, LANES)

    gt = bits > thr
    eq = bits == thr
    need = cap - count(gt)
    w_eq, o_eq = cumsum_incl(eq)
    sel = gt | (eq & (w_eq + o_eq <= need))
    within, rowoff = cumsum_incl(sel)
    pos = within + rowoff - 1.0
    pos_ref[0] = jnp.where(sel, pos, -1.0).astype(jnp.int32)
    gs_ref[0] = jnp.where(sel, aff, 0.0)
    rs_ref[0] = rowoff.astype(jnp.int32)

    s_col = lax.broadcasted_iota(jnp.int32, (cap, 1), 0).astype(F32)
    ones8 = jnp.ones((8, LANES), BF16)
    lane_r = lax.broadcasted_iota(jnp.int32, (cap, r), 1).astype(F32)
    for ei in range(e):
        sel_e = jnp.where(sel[ei], 1.0, 0.0).astype(BF16)
        rowtot_l = lax.dot_general(ones8, sel_e, (((1,), (1,)), ((), ())),
                                   preferred_element_type=F32)
        rowend_l = jnp.dot(rowtot_l.astype(BF16), ui_ref[...], preferred_element_type=F32)
        before = rowend_l[0:1, :] <= s_col
        row_s = jnp.sum(jnp.where(before, 1.0, 0.0), axis=1, keepdims=True)
        off_s = jnp.sum(jnp.where(before, rowtot_l[0:1, :], 0.0), axis=1, keepdims=True)
        onehot = jnp.where(lane_r == row_s, 1.0, 0.0).astype(BF16)
        w_rows = jnp.dot(onehot, within[ei].astype(BF16), preferred_element_type=F32)
        j_s = jnp.sum(jnp.where(w_rows <= s_col - off_s, 1.0, 0.0), axis=1, keepdims=True)
        idx_ref[0, ei] = jnp.broadcast_to(row_s * LANES + j_s, (cap, LANES)).astype(jnp.int32)


def route(logits, cap):
    b, e, t = logits.shape
    r = max(-(-t // LANES), 8)
    tp = r * LANES
    lg = jnp.pad(logits, ((0, 0), (0, 0), (0, tp - t))).reshape(b, e, r, LANES)
    tri = np.arange(LANES)[:, None] <= np.arange(LANES)[None, :]
    u = jnp.asarray(tri, BF16)
    ii = np.arange(e * r)
    lb = jnp.asarray((ii[:, None] // r == ii[None, :] // r) & (ii[None, :] % r < ii[:, None] % r), BF16)
    ui = jnp.asarray(np.arange(r)[:, None] <= np.arange(r)[None, :], BF16)
    blk = pl.BlockSpec((1, e, r, LANES), lambda bi: (bi, 0, 0, 0))
    pos, gs, idx, rs = pl.pallas_call(
        functools.partial(_route_body, cap=cap, t_valid=t),
        grid=(b,),
        in_specs=[blk,
                  pl.BlockSpec((LANES, LANES), lambda bi: (0, 0)),
                  pl.BlockSpec((e * r, e * r), lambda bi: (0, 0)),
                  pl.BlockSpec((r, r), lambda bi: (0, 0))],
        out_specs=(blk, blk, pl.BlockSpec((1, e, cap, LANES), lambda bi: (bi, 0, 0, 0)), blk),
        out_shape=(jax.ShapeDtypeStruct((b, e, r, LANES), jnp.int32),
                   jax.ShapeDtypeStruct((b, e, r, LANES), F32),
                   jax.ShapeDtypeStruct((b, e, cap, LANES), jnp.int32),
                   jax.ShapeDtypeStruct((b, e, r, LANES), jnp.int32)),
        compiler_params=_cparams("parallel"),
        name="route",
    )(lg, u, lb, ui)
    return (pos.reshape(b, e, tp)[:, :, :t], gs.reshape(b, e, tp)[:, :, :t], idx[..., 0], rs[..., 0])


def _mm_res_body(a_ref, w_ref, r_ref, g_ref, o_ref):
    acc = jnp.dot(a_ref[0], w_ref[...], preferred_element_type=F32)
    o_ref[0] = r_ref[0] + g_ref[0] * acc


def matmul_gated_residual(a, w, res, gate, *, tm=512, tn=2048):
    b, t, k = a.shape
    n = w.shape[1]
    tm = _row_tile(t, tm)
    tn = _row_tile(n, tn)
    return pl.pallas_call(
        _mm_res_body,
        grid=(b, t // tm, n // tn),
        in_specs=[
            pl.BlockSpec((1, tm, k), lambda bi, i, j: (bi, i, 0)),
            pl.BlockSpec((k, tn), lambda bi, i, j: (0, j)),
            pl.BlockSpec((1, tm, tn), lambda bi, i, j: (bi, i, j)),
            pl.BlockSpec((1, 1, tn), lambda bi, i, j: (bi, 0, j)),
        ],
        out_specs=pl.BlockSpec((1, tm, tn), lambda bi, i, j: (bi, i, j)),
        out_shape=jax.ShapeDtypeStruct((b, t, n), F32),
        compiler_params=_cparams("parallel", "parallel", "parallel"),
        name="matmul_gated_residual",
    )(a, w, res, gate)


def _rmsnorm_body(x_ref, g_ref, o_ref):
    x = x_ref[0]
    ms = jnp.mean(x * x, axis=-1, keepdims=True)
    o_ref[0] = (x * lax.rsqrt(ms + EPS)) * g_ref[...]


def rmsnorm_final(x, g, *, tm=512):
    b, t, d = x.shape
    tm = _row_tile(t, tm)
    return pl.pallas_call(
        _rmsnorm_body,
        grid=(b, t // tm),
        in_specs=[pl.BlockSpec((1, tm, d), lambda bi, i: (bi, i, 0)),
                  pl.BlockSpec((1, d), lambda bi, i: (0, 0))],
        out_specs=pl.BlockSpec((1, tm, d), lambda bi, i: (bi, i, 0)),
        out_shape=jax.ShapeDtypeStruct((b, t, d), F32),
        compiler_params=_cparams("parallel", "parallel"),
        name="rmsnorm_final",
    )(x, g.reshape(1, d))


def _na_body(q_ref, k0_ref, k1_ref, k2_ref, v0_ref, v1_ref, v2_ref, kc_ref, vc_ref, bias_ref, o_ref,
             *, heads, hd, qb):
    dn = (((1,), (1,)), ((), ()))
    k_refs = (k0_ref, k1_ref, k2_ref)
    v_refs = (v0_ref, v1_ref, v2_ref)
    for g in range(heads):
        sl = slice(g * hd, (g + 1) * hd)
        q = q_ref[0, :, sl]
        s = [lax.dot_general(q, k_refs[i][0, :, sl], dn, preferred_element_type=F32)
             + bias_ref[0, g, :, i * qb:(i + 1) * qb] for i in range(3)]
        s.append(lax.dot_general(q, kc_ref[0, :, sl], dn, preferred_element_type=F32))
        m = functools.reduce(jnp.maximum, [jnp.max(si, axis=-1, keepdims=True) for si in s])
        p = [jnp.exp(si - m) for si in s]
        l = functools.reduce(jnp.add, [jnp.sum(pi, axis=-1, keepdims=True) for pi in p])
        vals = [v_refs[i][0, :, sl] for i in range(3)] + [vc_ref[0, :, sl]]
        o = functools.reduce(jnp.add, [jnp.dot(pi.astype(BF16), vi, preferred_element_type=F32)
                                       for pi, vi in zip(p, vals)])
        o_ref[0, :, sl] = (o / l).astype(o_ref.dtype)


def _na_bias_table(rpb, rows):
    r_blk = NA_ROWS_PER_BLOCK
    nblk = rows // r_blk
    kr = min(NA_KR_MAX, rows)
    tables = []
    for rb in (0, 1, nblk - 1):
        ks = min(max(rb - 1, 0), nblk - 3) * r_blk
        q_row = rb * r_blk + np.arange(r_blk)
        rs = np.clip(q_row - kr // 2, 0, rows - kr)
        k_row = ks + np.arange(3 * r_blk)
        ok_row = (k_row[None, :] >= rs[:, None]) & (k_row[None, :] < rs[:, None] + kr)
        rel_row = np.clip(k_row[None, :] - q_row[:, None] + (NA_KR_MAX - 1), 0, 2 * NA_KR_MAX - 2)
        q_col = np.arange(GRID_W)
        cs = np.clip(q_col - NA_KC // 2, 0, GRID_W - NA_KC)
        k_col = np.arange(GRID_W)
        ok_col = (k_col[None, :] >= cs[:, None]) & (k_col[None, :] < cs[:, None] + NA_KC)
        rel_col = np.clip(k_col[None, :] - q_col[:, None] + (NA_KC - 1), 0, 2 * NA_KC - 2)
        hi = lax.Precision.HIGHEST
        oh_r = jnp.asarray(rel_row[:, :, None] == np.arange(2 * NA_KR_MAX - 1), F32)
        oh_c = jnp.asarray(np.arange(2 * NA_KC - 1)[:, None, None] == rel_col[None], F32)
        t = jnp.einsum('qkr,hrc->hqkc', oh_r, rpb, precision=hi)
        t = jnp.einsum('hqkc,cwv->hqwkv', t, oh_c, precision=hi)
        ok = ok_row[:, None, :, None] & ok_col[None, :, None, :]
        t = jnp.where(ok[None], t, MASK_VALUE)
        tables.append(t.reshape(rpb.shape[0], r_blk * GRID_W, 3 * r_blk * GRID_W))
    return jnp.stack(tables)


def neighbourhood_attention(qkv, qkv_ctx, rpb):
    b, t, d3 = qkv.shape
    d = d3 // 3
    l = qkv_ctx.shape[1]
    hd = d // NA_HEADS
    rows = t // GRID_W
    r_blk = NA_ROWS_PER_BLOCK
    qb = r_blk * GRID_W
    nblk = rows // r_blk
    assert rows % r_blk == 0 and nblk >= 3 and min(NA_KR_MAX, rows) + r_blk <= 3 * r_blk
    g = NA_HEADS_PER_STEP
    gw = g * hd
    nhg = NA_HEADS // g
    bias = _na_bias_table(rpb.astype(F32), rows)

    def kstart(rb):
        return jnp.clip(rb - 1, 0, nblk - 3)

    def pattern(rb):
        return jnp.where(rb == 0, 0, jnp.where(rb == nblk - 1, 2, 1))

    q_spec = pl.BlockSpec((1, qb, gw), lambda hg, bi, rb: (bi, rb, hg))
    k_specs = [pl.BlockSpec((1, qb, gw), functools.partial(
        lambda hg, bi, rb, i: (bi, kstart(rb) + i, nhg + hg), i=i)) for i in range(3)]
    v_specs = [pl.BlockSpec((1, qb, gw), functools.partial(
        lambda hg, bi, rb, i: (bi, kstart(rb) + i, 2 * nhg + hg), i=i)) for i in range(3)]
    kc_spec = pl.BlockSpec((1, l, gw), lambda hg, bi, rb: (bi, 0, nhg + hg))
    vc_spec = pl.BlockSpec((1, l, gw), lambda hg, bi, rb: (bi, 0, 2 * nhg + hg))
    bias_spec = pl.BlockSpec((1, g, qb, 3 * qb), lambda hg, bi, rb: (pattern(rb), hg, 0, 0))
    return pl.pallas_call(
        functools.partial(_na_body, heads=g, hd=hd, qb=qb),
        grid=(nhg, b, nblk),
        in_specs=[q_spec] + k_specs + v_specs + [kc_spec, vc_spec, bias_spec],
        out_specs=pl.BlockSpec((1, qb, gw), lambda hg, bi, rb: (bi, rb, hg)),
        out_shape=jax.ShapeDtypeStruct((b, t, d), BF16),
        compiler_params=_cparams("parallel", "parallel", "arbitrary"),
        name="neighbourhood_attention",
    )(qkv, qkv, qkv, qkv, qkv, qkv, qkv, qkv_ctx, qkv_ctx, bias)


MOE_GATHER_UNROLL = 8
MOE_ISSUE_GROUPS = 8


def _moe_ffn_body(first_ref, next_ref, h_hbm, w1_ref, w3_ref, w2_ref, o_ref, xbuf, xb_scr, sem):
    i = pl.program_id(1)
    n_tiles = pl.num_programs(1)
    step = pl.program_id(0) * n_tiles + i
    tm = xbuf.shape[1]
    slot = lax.rem(step, 2)

    def row_copy(src_row, sl, r):
        return pltpu.make_async_copy(h_hbm.at[pl.ds(src_row, 1)], xbuf.at[sl, pl.ds(r, 1)], sem.at[sl])

    @pl.when(step == 0)
    def _():
        def body(r, carry):
            row_copy(first_ref[0, 0, r], slot, r).start()
            return carry
        lax.fori_loop(0, tm, body, 0, unroll=MOE_GATHER_UNROLL)

    pltpu.make_async_copy(h_hbm.at[pl.ds(0, tm)], xbuf.at[slot], sem.at[slot]).wait()
    xb_scr[...] = xbuf[slot].astype(BF16)

    per_group = tm // MOE_ISSUE_GROUPS

    def issue_group(gi):
        for r in range(gi * per_group, (gi + 1) * per_group):
            row_copy(next_ref[0, 0, i * tm + r], 1 - slot, r).start()

    n_chunks = MOE_ISSUE_GROUPS // 2
    fc = w1_ref.shape[2] // n_chunks
    parts = []
    for c in range(n_chunks):
        cols = slice(c * fc, (c + 1) * fc)
        issue_group(2 * c)
        a = jnp.dot(xb_scr[...], w1_ref[0, :, cols], preferred_element_type=F32)
        issue_group(2 * c + 1)
        bb = jnp.dot(xb_scr[...], w3_ref[0, :, cols], preferred_element_type=F32)
        parts.append((a * jax.nn.sigmoid(a) * bb).astype(BF16))
    hm = jnp.concatenate(parts, axis=1)
    o_ref[0] = jnp.dot(hm, w2_ref[0], preferred_element_type=F32).astype(o_ref.dtype)

    @pl.when(step == pl.num_programs(0) * n_tiles - 1)
    def _():
        pltpu.make_async_copy(h_hbm.at[pl.ds(0, tm)], xbuf.at[1 - slot], sem.at[1 - slot]).wait()


def moe_ffn(h, rows, w1, w3, w2, *, tm=256):
    e, m = rows.shape
    d = h.shape[1]
    f = w1.shape[2]
    tm = _row_tile(m, tm)
    assert tm % MOE_ISSUE_GROUPS == 0 and f % (MOE_ISSUE_GROUPS // 2 * LANES) == 0
    rows_next = jnp.roll(rows.reshape(-1), -tm).reshape(e, 1, m)
    rows_first = rows[0, :tm].reshape(1, 1, tm)
    return pl.pallas_call(
        _moe_ffn_body,
        grid=(e, m // tm),
        in_specs=[
            pl.BlockSpec((1, 1, tm), lambda ei, i: (0, 0, 0), memory_space=pltpu.SMEM),
            pl.BlockSpec((1, 1, m), lambda ei, i: (ei, 0, 0), memory_space=pltpu.SMEM),
            pl.BlockSpec(memory_space=pl.ANY),
            pl.BlockSpec((1, d, f), lambda ei, i: (ei, 0, 0)),
            pl.BlockSpec((1, d, f), lambda ei, i: (ei, 0, 0)),
            pl.BlockSpec((1, f, d), lambda ei, i: (ei, 0, 0)),
        ],
        out_specs=pl.BlockSpec((1, tm, d), lambda ei, i: (ei, i, 0)),
        out_shape=jax.ShapeDtypeStruct((e, m, d), BF16),
        scratch_shapes=[pltpu.VMEM((2, tm, d), F32), pltpu.VMEM((tm, d), BF16),
                        pltpu.SemaphoreType.DMA((2,))],
        compiler_params=_cparams("arbitrary", "arbitrary"),
        name="moe_ffn",
    )(rows_first, rows_next, h, w1, w3, w2)


COMBINE_TOKENS = 256
COMBINE_WINDOW = 64
BF16_SUBLANES = 16


def _combine_body(ts_ref, pos_ref, gs_ref, x_ref, g2_ref, ye_hbm, o_ref, ybuf, yextra, sem, sem_x,
                  *, n_exp, cap, win, n_batch):
    bi = pl.program_id(0)
    ti = pl.program_id(1)
    nt = pl.num_programs(1)
    step = bi * nt + ti
    total = n_batch * nt
    slot = lax.rem(step, 2)
    tt = pos_ref.shape[2]

    def tile_start(b, t, e):
        return ts_ref[(b * (nt + 1) + t) * n_exp + e]

    def window_start(b, t, e, w):
        st = tile_start(b, t, e)
        a = lax.shift_left(lax.shift_right_logical(st, 4), 4) + w * win
        return jnp.minimum(a, cap - win)

    def window_copy(b, e, a, dst, dsem):
        row0 = pl.multiple_of((e * n_batch + b) * cap + a, BF16_SUBLANES)
        return pltpu.make_async_copy(ye_hbm.at[pl.ds(row0, win)], dst, dsem)

    def issue(b, t, sl):
        for e in range(n_exp):
            window_copy(b, e, window_start(b, t, e, 0), ybuf.at[sl, e], sem.at[sl]).start()

    @pl.when(step == 0)
    def _():
        issue(bi, ti, slot)

    for e in range(n_exp):
        window_copy(bi, e, 0, ybuf.at[slot, e], sem.at[slot]).wait()

    @pl.when(step + 1 < total)
    def _():
        nxt = step + 1
        issue(nxt // nt, lax.rem(nxt, nt), 1 - slot)

    m_iota = lax.broadcasted_iota(jnp.int32, (win, tt), 0)
    dn = (((0,), (0,)), ((), ()))

    def gather_matrix(w):
        rows = []
        for e in range(n_exp):
            a = window_start(bi, ti, e, w)
            lo = window_start(bi, ti, e, 0) + w * win
            p = pos_ref[0, e:e + 1, :]
            hit = (p - a == m_iota) & (p >= lo)
            rows.append(jnp.where(hit, gs_ref[0, e:e + 1, :], 0.0).astype(BF16))
        return jnp.concatenate(rows, axis=0)

    d = ybuf.shape[-1]
    acc = lax.dot_general(gather_matrix(0), ybuf[slot].reshape(n_exp * win, d), dn,
                          preferred_element_type=F32)

    n_win = jnp.int32(1)
    for e in range(n_exp):
        span = tile_start(bi, ti + 1, e) - window_start(bi, ti, e, 0)
        n_win = jnp.maximum(n_win, (span + win - 1) // win)

    def extra(w, acc):
        for e in range(n_exp):
            cp = window_copy(bi, e, window_start(bi, ti, e, w), yextra.at[e], sem_x)
            cp.start()
            cp.wait()
        return acc + lax.dot_general(gather_matrix(w), yextra[...].reshape(n_exp * win, d), dn,
                                     preferred_element_type=F32)

    acc = lax.fori_loop(1, n_win, extra, acc)
    o_ref[0] = x_ref[0] + g2_ref[0] * acc


def moe_combine(x, g2, ye, pos, gs, tile_start, cap):
    b, t, d = x.shape
    e = pos.shape[1]
    tt = min(COMBINE_TOKENS, t)
    win = min(COMBINE_WINDOW, cap)
    nt = t // tt
    grid_spec = pltpu.PrefetchScalarGridSpec(
        num_scalar_prefetch=1,
        grid=(b, nt),
        in_specs=[
            pl.BlockSpec((1, e, tt), lambda bi, ti, ts: (bi, 0, ti)),
            pl.BlockSpec((1, e, tt), lambda bi, ti, ts: (bi, 0, ti)),
            pl.BlockSpec((1, tt, d), lambda bi, ti, ts: (bi, ti, 0)),
            pl.BlockSpec((1, 1, d), lambda bi, ti, ts: (bi, 0, 0)),
            pl.BlockSpec(memory_space=pl.ANY),
        ],
        out_specs=pl.BlockSpec((1, tt, d), lambda bi, ti, ts: (bi, ti, 0)),
        scratch_shapes=[
            pltpu.VMEM((2, e, win, d), BF16),
            pltpu.VMEM((e, win, d), BF16),
            pltpu.SemaphoreType.DMA((2,)),
            pltpu.SemaphoreType.DMA(()),
        ],
    )
    return pl.pallas_call(
        functools.partial(_combine_body, n_exp=e, cap=cap, win=win, n_batch=b),
        grid_spec=grid_spec,
        out_shape=jax.ShapeDtypeStruct((b, t, d), F32),
        compiler_params=_cparams("arbitrary", "arbitrary"),
        name="moe_combine",
    )(tile_start, pos, gs, x, g2, ye)


def expert_choice_moe(x, g, shift, scale, g2, router_w, w1, w3, w2):
    b, t, d = x.shape
    e = router_w.shape[1]
    cap = EC_CAPACITY_FACTOR * t // e
    h, logits = norm_mod_router(x, g, shift, scale, router_w)
    pos, gs, idx, rowstart = route(logits, cap)
    rows = idx + (jnp.arange(b, dtype=jnp.int32) * t)[:, None, None]
    rows = jnp.swapaxes(rows, 0, 1).reshape(e, b * cap)
    ye = moe_ffn(h.reshape(b * t, d), rows, w1, w3, w2).reshape(e * b * cap, d)
    tt = min(COMBINE_TOKENS, t)
    ts = rowstart[:, :, ::tt // LANES][:, :, :t // tt]
    ts = jnp.concatenate([jnp.swapaxes(ts, 1, 2), jnp.full((b, 1, e), cap, jnp.int32)], axis=1)
    return moe_combine(x, g2, ye, pos, gs, ts.reshape(-1), cap)


def _axial_rope_angles(n_tok, head_dim):
    n_freq = head_dim // 4
    inv = ROPE_BASE ** (-jnp.arange(n_freq, dtype=F32) / n_freq)
    tt = jnp.arange(n_tok)
    row = (tt // GRID_W).astype(F32)
    col = (tt % GRID_W).astype(F32)
    return row[:, None] * inv[None], col[:, None] * inv[None]


GLA_BLOCK = 4 * GLA_CHUNK
GLA_HEADS_PER_STEP = 2


def _log_sigmoid(x):
    return jnp.minimum(x, 0.0) - jnp.log(1.0 + jnp.exp(-jnp.abs(x)))


def _gla_block(q_ref, k_ref, v_ref, lr_ref, cos_ref, sin_ref, a_ref, ab_ref, tri_ref, perm_ref, st_ref, o_ref,
               *, reverse, g, hk, hv):
    ks = slice(g * hk, (g + 1) * hk)
    vs = slice(g * hv, (g + 1) * hv)
    nt = (((1,), (1,)), ((), ()))
    tn = (((0,), (0,)), ((), ()))
    hi = lax.Precision.HIGHEST
    n = GLA_BLOCK
    c = GLA_CHUNK
    pre = jnp.dot(lr_ref[0], a_ref[0, :, ks], preferred_element_type=F32, precision=hi) + ab_ref[0, :, ks]
    la = _log_sigmoid(pre) / GLA_GATE_NORM
    la_hi = la.astype(BF16)
    la_r = la - la_hi.astype(F32)
    la_mid = la_r.astype(BF16)
    la_lo = (la_r - la_mid.astype(F32)).astype(BF16)
    tri = tri_ref[...]
    bcum = (jnp.dot(tri, la_hi, preferred_element_type=F32) + jnp.dot(tri, la_mid, preferred_element_type=F32)
            + jnp.dot(tri, la_lo, preferred_element_type=F32))
    q = q_ref[0, :, ks]
    k = k_ref[0, :, ks]
    v = v_ref[0, :, vs]
    qf = q.astype(F32)
    kf = k.astype(F32)
    if cos_ref is not None:
        cs, sn = cos_ref[...], sin_ref[...]
        qf = qf * cs + jnp.dot(q, perm_ref[...], preferred_element_type=F32) * sn
        kf = kf * cs + jnp.dot(k, perm_ref[...], preferred_element_type=F32) * sn
    last_row = [bcum[(i * c if reverse else i * c + c - 1):(i * c + 1 if reverse else i * c + c)]
                for i in range(n // c)]
    blast = jnp.concatenate([jnp.broadcast_to(r, (c, r.shape[1])) for r in last_row], axis=0)
    qe = (qf * jnp.exp(bcum)).astype(BF16)
    ke = (kf * jnp.exp(-bcum)).astype(BF16)
    kd = (kf * jnp.exp(blast - bcum)).astype(BF16)
    row = lax.broadcasted_iota(jnp.int32, (n, n), 0)
    col = lax.broadcasted_iota(jnp.int32, (n, n), 1)
    same_chunk = (row // c) == (col // c)
    causal = (col >= row) if reverse else (col <= row)
    att = jnp.where(same_chunk & causal, lax.dot_general(qe, ke, nt, preferred_element_type=F32), 0.0)
    o_intra = jnp.dot(att.astype(BF16), v, preferred_element_type=F32)
    order = range(n // c - 1, -1, -1) if reverse else range(n // c)
    for i in order:
        sl = slice(i * c, (i + 1) * c)
        st = st_ref[g]
        o_ref[0, 0, sl, vs] = o_intra[sl] + lax.dot_general(qe[sl], st.astype(BF16), nt,
                                                            preferred_element_type=F32)
        u_t = lax.dot_general(v[sl], kd[sl], tn, preferred_element_type=F32)
        st_ref[g] = st * jnp.exp(last_row[i]) + u_t


def _gla_body(qc_ref, kc_ref, vc_ref, lrc_ref, ql_ref, kl_ref, vl_ref, lrl_ref, cos_ref, sin_ref,
              a_ref, ab_ref, tri_ref, perm_ref, oc_ref, ol_ref, st_ref, *, reverse, heads, hk, hv):
    j = pl.program_id(2)

    @pl.when(j == 0)
    def _():
        st_ref[...] = jnp.zeros_like(st_ref)
        for g in range(heads):
            _gla_block(qc_ref, kc_ref, vc_ref, lrc_ref, None, None, a_ref, ab_ref, tri_ref, perm_ref, st_ref,
                       oc_ref, reverse=reverse, g=g, hk=hk, hv=hv)

    @pl.when(j > 0)
    def _():
        for g in range(heads):
            _gla_block(ql_ref, kl_ref, vl_ref, lrl_ref, cos_ref, sin_ref, a_ref, ab_ref, tri_ref, perm_ref,
                       st_ref, ol_ref, reverse=reverse, g=g, hk=hk, hv=hv)


def gla_scan(p_c, lr_c, p_l, lr_l, a_up, a_b, cos, sin, *, dk, dv, reverse):
    b, tc, _ = p_c.shape
    tl = p_l.shape[1]
    hk, hv = dk // GLA_HEADS, dv // GLA_HEADS
    n = GLA_BLOCK
    assert tc == n and tl % n == 0 and hk == LANES
    nl = tl // n
    hps = GLA_HEADS_PER_STEP
    kw, vw = hps * hk, hps * hv
    qo, ko, vo = 0, dk // kw, 2 * dk // vw
    ii = np.arange(n)
    same = ii[:, None] // GLA_CHUNK == ii[None, :] // GLA_CHUNK
    tri = jnp.asarray(same & ((ii[None, :] >= ii[:, None]) if reverse else (ii[None, :] <= ii[:, None])), BF16)
    jj = np.arange(hk)
    partner = np.where((jj % (hk // 2)) < hk // 4, jj + hk // 4, jj - hk // 4)
    perm = np.zeros((hk, hk), np.float32)
    perm[partner, jj] = 1.0
    perm = jnp.asarray(perm, BF16)

    def lat_blk(j):
        blk = jnp.clip(j - 1, 0, nl - 1)
        return (nl - 1 - blk) if reverse else blk

    ctx_spec = lambda width, off: pl.BlockSpec((1, n, width), lambda bi, h, j: (bi, 0, off + h))
    lat_spec = lambda width, off: pl.BlockSpec((1, n, width), lambda bi, h, j: (bi, lat_blk(j), off + h))
    in_specs = [
        ctx_spec(kw, qo), ctx_spec(kw, ko), ctx_spec(vw, vo),
        pl.BlockSpec((1, n, LANES), lambda bi, h, j: (bi, 0, 0)),
        lat_spec(kw, qo), lat_spec(kw, ko), lat_spec(vw, vo),
        pl.BlockSpec((1, n, LANES), lambda bi, h, j: (bi, lat_blk(j), 0)),
        pl.BlockSpec((n, hk), lambda bi, h, j: (lat_blk(j), 0)),
        pl.BlockSpec((n, hk), lambda bi, h, j: (lat_blk(j), 0)),
        pl.BlockSpec((1, LANES, kw), lambda bi, h, j: (0, 0, h)),
        pl.BlockSpec((1, 1, kw), lambda bi, h, j: (0, 0, h)),
        pl.BlockSpec((n, n), lambda bi, h, j: (0, 0)),
        pl.BlockSpec((hk, hk), lambda bi, h, j: (0, 0)),
    ]
    out_specs = (
        pl.BlockSpec((1, 1, n, vw), lambda bi, h, j: (0, bi, 0, h)),
        pl.BlockSpec((1, 1, n, vw), lambda bi, h, j: (0, bi, lat_blk(j), h)),
    )
    o_c, o_l = pl.pallas_call(
        functools.partial(_gla_body, reverse=reverse, heads=hps, hk=hk, hv=hv),
        grid=(b, GLA_HEADS // hps, nl + 1),
        in_specs=in_specs,
        out_specs=out_specs,
        out_shape=(jax.ShapeDtypeStruct((1, b, tc, dv), F32), jax.ShapeDtypeStruct((1, b, tl, dv), F32)),
        scratch_shapes=[pltpu.VMEM((hps, hv, hk), F32)],
        compiler_params=_cparams("parallel", "parallel", "arbitrary"),
        name="gla_scan_bwd" if reverse else "gla_scan_fwd",
    )(p_c, p_c, p_c, lr_c, p_l, p_l, p_l, lr_l, cos, sin, a_up[None], a_b[None], tri, perm)
    return o_c[0], o_l[0]


def _gla_out_body(of_ref, ob_ref, g_ref, ng_ref, y_ref, *, heads):
    o = of_ref[0] + ob_ref[0]
    hv = o.shape[1] // heads
    outs = []
    for h in range(heads):
        oh = o[:, h * hv:(h + 1) * hv]
        outs.append(oh * lax.rsqrt(jnp.mean(oh * oh, axis=-1, keepdims=True) + EPS))
    on = jnp.concatenate(outs, axis=1) * ng_ref[...]
    gg = g_ref[0].astype(F32)
    y_ref[0] = (on * (gg * jax.nn.sigmoid(gg))).astype(y_ref.dtype)


def gla_output(o_f, o_b, p, norm_g, *, g_off, tm=256):
    b, t, dv = o_f.shape
    tm = _row_tile(t, tm)
    blk = pl.BlockSpec((1, tm, dv), lambda bi, i: (bi, i, 0))
    return pl.pallas_call(
        functools.partial(_gla_out_body, heads=GLA_HEADS),
        grid=(b, t // tm),
        in_specs=[blk, blk, pl.BlockSpec((1, tm, dv), lambda bi, i: (bi, i, g_off // dv)),
                  pl.BlockSpec((1, dv), lambda bi, i: (0, 0))],
        out_specs=blk,
        out_shape=jax.ShapeDtypeStruct((b, t, dv), BF16),
        compiler_params=_cparams("parallel", "parallel"),
        name="gla_output",
    )(o_f, o_b, p, norm_g.reshape(1, dv))


FFT_T2 = 128
FFT_ROWS = 8


def _dft_mats(n):
    ang = 2.0 * np.pi * ((np.arange(n)[:, None] * np.arange(n)[None, :]) % n) / n
    return np.cos(ang), np.sin(ang)


def _chan_dft_body(u_ref, w_ref, z_ref):
    z_ref[0] = jnp.dot(u_ref[0], w_ref[...], preferred_element_type=F32)


def channel_dft(p, *, u_off, fd, tm=512):
    b, t, _ = p.shape
    gd = fd // FNET_GROUPS
    c, s = _dft_mats(gd)
    w = jnp.asarray(np.concatenate([c, -s], axis=1), BF16)
    tm = _row_tile(t, tm)
    return pl.pallas_call(
        _chan_dft_body,
        grid=(b, t // tm, FNET_GROUPS),
        in_specs=[pl.BlockSpec((1, tm, gd), lambda bi, i, g: (bi, i, u_off // gd + g)),
                  pl.BlockSpec((gd, 2 * gd), lambda bi, i, g: (0, 0))],
        out_specs=pl.BlockSpec((1, tm, 2 * gd), lambda bi, i, g: (bi, i, g)),
        out_shape=jax.ShapeDtypeStruct((b, t, 2 * fd), F32),
        compiler_params=_cparams("parallel", "parallel", "parallel"),
        name="channel_dft",
    )(p, w)


def _dft_direct_body(z_ref, f_ref, y_ref, *, gd, scale):
    z = z_ref[0]
    zz = jnp.concatenate([z[:, :gd], z[:, gd:]], axis=0).astype(BF16)
    y_ref[0] = jnp.dot(f_ref[...], zz, preferred_element_type=F32) * scale


def token_dft_direct(z, gd):
    b, t, w2 = z.shape
    c, s = _dft_mats(t)
    f = jnp.asarray(np.concatenate([c, s], axis=1), BF16)
    return pl.pallas_call(
        functools.partial(_dft_direct_body, gd=gd, scale=float((t * gd) ** -0.5)),
        grid=(b, FNET_GROUPS),
        in_specs=[pl.BlockSpec((1, t, 2 * gd), lambda bi, g: (bi, 0, g)),
                  pl.BlockSpec((t, 2 * t), lambda bi, g: (0, 0))],
        out_specs=pl.BlockSpec((1, t, gd), lambda bi, g: (bi, 0, g)),
        out_shape=jax.ShapeDtypeStruct((b, t, w2 // 2), F32),
        compiler_params=_cparams("parallel", "parallel"),
        name="token_dft_direct",
    )(z, f)


def _dft_stage1_body(z_ref, f_ref, twc_ref, tws_ref, o_ref, *, gd):
    for j in range(FFT_ROWS):
        z = z_ref[0, :, j, :]
        zz = jnp.concatenate([z[:, :gd], z[:, gd:]], axis=0).astype(BF16)
        a = jnp.dot(f_ref[...], zz, preferred_element_type=F32)
        t1 = a.shape[0] // 2
        ar, ai = a[:t1], a[t1:]
        tc = jnp.concatenate([twc_ref[j]] * (gd // LANES), axis=1)
        ts = jnp.concatenate([tws_ref[j]] * (gd // LANES), axis=1)
        o_ref[0, 0, :, j, :gd] = ar * tc + ai * ts
        o_ref[0, 0, :, j, gd:] = ai * tc - ar * ts


def _dft_stage2_body(b_ref, f_ref, y_ref, *, gd, scale):
    for j in range(FFT_ROWS):
        s = b_ref[0, 0, j]
        ss = jnp.concatenate([s[:, :gd], s[:, gd:]], axis=0).astype(BF16)
        y_ref[0, :, j, :] = jnp.dot(f_ref[...], ss, preferred_element_type=F32) * scale


def token_dft_two_stage(z, gd):
    b, t, w2 = z.shape
    t2 = FFT_T2
    t1 = t // t2
    assert t1 * t2 == t and t1 % FFT_ROWS == 0 and t2 % FFT_ROWS == 0 and gd % LANES == 0
    c1, s1 = _dft_mats(t1)
    f1 = jnp.asarray(np.block([[c1, s1], [-s1, c1]]), BF16)
    ang = 2.0 * np.pi * (np.arange(t2)[:, None] * np.arange(t1)[None, :]) / t
    twc = jnp.asarray(np.broadcast_to(np.cos(ang)[:, :, None], (t2, t1, LANES)), F32)
    tws = jnp.asarray(np.broadcast_to(np.sin(ang)[:, :, None], (t2, t1, LANES)), F32)
    c2, s2 = _dft_mats(t2)
    f2 = jnp.asarray(np.concatenate([c2, s2], axis=1), BF16)
    z4 = z.reshape(b, t1, t2, w2)
    bmid = pl.pallas_call(
        functools.partial(_dft_stage1_body, gd=gd),
        grid=(b, FNET_GROUPS, t2 // FFT_ROWS),
        in_specs=[pl.BlockSpec((1, t1, FFT_ROWS, 2 * gd), lambda bi, g, i: (bi, 0, i, g)),
                  pl.BlockSpec((2 * t1, 2 * t1), lambda bi, g, i: (0, 0)),
                  pl.BlockSpec((FFT_ROWS, t1, LANES), lambda bi, g, i: (i, 0, 0)),
                  pl.BlockSpec((FFT_ROWS, t1, LANES), lambda bi, g, i: (i, 0, 0))],
        out_specs=pl.BlockSpec((1, 1, t1, FFT_ROWS, 2 * gd), lambda bi, g, i: (bi, g, 0, i, 0)),
        out_shape=jax.ShapeDtypeStruct((b, FNET_GROUPS, t1, t2, 2 * gd), F32),
        compiler_params=_cparams("parallel", "parallel", "parallel"),
        name="token_dft_stage1",
    )(z4, f1, twc, tws)
    y = pl.pallas_call(
        functools.partial(_dft_stage2_body, gd=gd, scale=float((t * gd) ** -0.5)),
        grid=(b, FNET_GROUPS, t1 // FFT_ROWS),
        in_specs=[pl.BlockSpec((1, 1, FFT_ROWS, t2, 2 * gd), lambda bi, g, i: (bi, g, i, 0, 0)),
                  pl.BlockSpec((t2, 2 * t2), lambda bi, g, i: (0, 0))],
        out_specs=pl.BlockSpec((1, t2, FFT_ROWS, gd), lambda bi, g, i: (bi, 0, i, g)),
        out_shape=jax.ShapeDtypeStruct((b, t2, t1, w2 // 2), F32),
        compiler_params=_cparams("parallel", "parallel", "parallel"),
        name="token_dft_stage2",
    )(bmid, f2)
    return y.reshape(b, t, w2 // 2)


def fourier_mix(p, *, u_off, fd):
    gd = fd // FNET_GROUPS
    z = channel_dft(p, u_off=u_off, fd=fd)
    if z.shape[1] <= 512:
        return token_dft_direct(z, gd)
    return token_dft_two_stage(z, gd)


def _mm2_res_body(a1_ref, a2_ref, w1_ref, w2_ref, r_ref, g_ref, o_ref):
    acc = jnp.dot(a1_ref[0].astype(BF16), w1_ref[...], preferred_element_type=F32)
    acc = acc + jnp.dot(a2_ref[0].astype(BF16), w2_ref[...], preferred_element_type=F32)
    o_ref[0] = r_ref[0] + g_ref[0] * acc


def matmul2_gated_residual(a1, a2, w, res, gate, *, tm=512, tn=2048):
    b, t, k1 = a1.shape
    k2 = a2.shape[2]
    n = w.shape[1]
    tm = _row_tile(t, tm)
    tn = _row_tile(n, tn)
    return pl.pallas_call(
        _mm2_res_body,
        grid=(b, t // tm, n // tn),
        in_specs=[
            pl.BlockSpec((1, tm, k1), lambda bi, i, j: (bi, i, 0)),
            pl.BlockSpec((1, tm, k2), lambda bi, i, j: (bi, i, 0)),
            pl.BlockSpec((k1, tn), lambda bi, i, j: (0, j)),
            pl.BlockSpec((k2, tn), lambda bi, i, j: (k1 // k2, j)),
            pl.BlockSpec((1, tm, tn), lambda bi, i, j: (bi, i, j)),
            pl.BlockSpec((1, 1, tn), lambda bi, i, j: (bi, 0, j)),
        ],
        out_specs=pl.BlockSpec((1, tm, tn), lambda bi, i, j: (bi, i, j)),
        out_shape=jax.ShapeDtypeStruct((b, t, n), F32),
        compiler_params=_cparams("parallel", "parallel", "parallel"),
        name="matmul2_gated_residual",
    )(a1, a2, w, w, res, gate)


def mixer_gla_fnet(x_ctx, x_lat, mod_ctx, mod_lat, norm_g1, w_in, w_out,
                   a_up_f, a_b_f, a_up_b, a_b_b, gla_norm_g, last):
    dv = gla_norm_g.shape[0]
    dk = a_up_f.shape[1]
    t = x_lat.shape[1]
    hk = dk // GLA_HEADS
    n_main = 2 * dk + 2 * dv
    fd = w_in.shape[1] - n_main - 2 * GLA_RANK
    w_main = jnp.concatenate([w_in[:, :n_main], w_in[:, n_main + 2 * GLA_RANK:]], axis=1).astype(BF16)
    w_lr = jnp.pad(w_in[:, n_main:n_main + 2 * GLA_RANK], ((0, 0), (0, LANES - 2 * GLA_RANK))).astype(BF16)
    cs = jnp.concatenate([jnp.full((dk,), hk ** -0.5, F32), jnp.ones((w_main.shape[1] - dk,), F32)])
    p_l, lr_l = norm_mod_matmul(x_lat, norm_g1, mod_lat[0], mod_lat[1], w_main, cs, w_aux=w_lr)
    p_c, lr_c = norm_mod_matmul(x_ctx, norm_g1, mod_ctx[0], mod_ctx[1], w_main, cs, w_aux=w_lr)
    ang_row, ang_col = _axial_rope_angles(t, hk)
    cos = jnp.concatenate([jnp.cos(ang_row)] * 2 + [jnp.cos(ang_col)] * 2, axis=1)
    sin = jnp.concatenate([-jnp.sin(ang_row), jnp.sin(ang_row), -jnp.sin(ang_col), jnp.sin(ang_col)], axis=1)
    pad_f = ((0, LANES - GLA_RANK), (0, 0))
    pad_b = ((GLA_RANK, LANES - 2 * GLA_RANK), (0, 0))
    ofc, ofl = gla_scan(p_c, lr_c, p_l, lr_l, jnp.pad(a_up_f, pad_f), a_b_f[None], cos, sin,
                        dk=dk, dv=dv, reverse=False)
    obc, obl = gla_scan(p_c, lr_c, p_l, lr_l, jnp.pad(a_up_b, pad_b), a_b_b[None], cos, sin,
                        dk=dk, dv=dv, reverse=True)
    w_out_b = w_out.astype(BF16)
    g_off, u_off = 2 * dk + dv, 2 * dk + 2 * dv
    x_lat = matmul2_gated_residual(gla_output(ofl, obl, p_l, gla_norm_g, g_off=g_off),
                                   fourier_mix(p_l, u_off=u_off, fd=fd), w_out_b, x_lat, mod_lat[2])
    if not last:
        x_ctx = matmul2_gated_residual(gla_output(ofc, obc, p_c, gla_norm_g, g_off=g_off),
                                       fourier_mix(p_c, u_off=u_off, fd=fd), w_out_b, x_ctx, mod_ctx[2])
    return x_ctx, x_lat


def _ctx_attn_body(q_ref, k_ref, v_ref, o_ref, *, heads, hd):
    nt = (((1,), (1,)), ((), ()))
    for h in range(heads):
        sl = slice(h * hd, (h + 1) * hd)
        s = lax.dot_general(q_ref[0, :, sl], k_ref[0, :, sl], nt, preferred_element_type=F32)
        p = jnp.exp(s - jnp.max(s, axis=-1, keepdims=True))
        o = jnp.dot(p.astype(BF16), v_ref[0, :, sl], preferred_element_type=F32)
        o_ref[0, :, sl] = (o / jnp.sum(p, axis=-1, keepdims=True)).astype(o_ref.dtype)


def context_self_attention(qkv_ctx):
    b, l, d3 = qkv_ctx.shape
    d = d3 // 3
    hd = d // NA_HEADS
    g = NA_HEADS_PER_STEP
    gw = g * hd
    nhg = NA_HEADS // g
    return pl.pallas_call(
        functools.partial(_ctx_attn_body, heads=g, hd=hd),
        grid=(b, nhg),
        in_specs=[pl.BlockSpec((1, l, gw), lambda bi, hg: (bi, 0, hg)),
                  pl.BlockSpec((1, l, gw), lambda bi, hg: (bi, 0, nhg + hg)),
                  pl.BlockSpec((1, l, gw), lambda bi, hg: (bi, 0, 2 * nhg + hg))],
        out_specs=pl.BlockSpec((1, l, gw), lambda bi, hg: (bi, 0, hg)),
        out_shape=jax.ShapeDtypeStruct((b, l, d), BF16),
        compiler_params=_cparams("parallel", "parallel"),
        name="context_self_attention",
    )(qkv_ctx, qkv_ctx, qkv_ctx)


def mixer_neighbourhood(x_ctx, x_lat, mod_ctx, mod_lat, norm_g1, w_qkv, w_out, rpb, last):
    bsz, t, d = x_lat.shape
    l = x_ctx.shape[1]
    hd = d // NA_HEADS
    w_qkv_b = w_qkv.astype(BF16)
    w_out_b = w_out.astype(BF16)
    cs = jnp.concatenate([jnp.full((d,), hd ** -0.5, F32), jnp.ones((2 * d,), F32)])
    qkv_l = norm_mod_matmul(x_lat, norm_g1, mod_lat[0], mod_lat[1], w_qkv_b, cs)
    qkv_c = norm_mod_matmul(x_ctx, norm_g1, mod_ctx[0], mod_ctx[1], w_qkv_b, cs)
    o_l = neighbourhood_attention(qkv_l, qkv_c, rpb)
    x_lat = matmul_gated_residual(o_l, w_out_b, x_lat, mod_lat[2])
    if not last:
        x_ctx = matmul_gated_residual(context_self_attention(qkv_c), w_out_b, x_ctx, mod_ctx[2])
    return x_ctx, x_lat


def kernel(x, c, ctx, c_ctx, ada_w, ada_b, norm1_g, norm2_g, ab_w_in, ab_w_out, gla_a_up_f, gla_a_b_f,
           gla_a_up_b, gla_a_b_b, gla_norm_g, na_w_qkv, na_w_out, na_rpb, router_w, exp_w1, exp_w3,
           exp_w2, final_g):
    depth = ada_w.shape[0]
    bsz, t, d = x.shape
    cond = jnp.concatenate([c, c_ctx[None], jnp.zeros((8 - bsz - 1, d), F32)], axis=0)
    mods = ada_all_layers(cond, ada_w, ada_b)
    xl, xc = x, ctx
    for l in range(depth):
        last = l == depth - 1
        m = mods[l].reshape(8, N_MOD, d)
        mod_lat = [m[:bsz, i][:, None, :] for i in range(N_MOD)]
        mod_ctx = [jnp.broadcast_to(m[bsz, i][None, None, :], (bsz, 1, d)) for i in range(N_MOD)]
        if l % 2 == 0:
            e = l // 2
            xc, xl = mixer_gla_fnet(xc, xl, mod_ctx[:3], mod_lat[:3], norm1_g[l], ab_w_in[e], ab_w_out[e],
                                    gla_a_up_f[e], gla_a_b_f[e], gla_a_up_b[e], gla_a_b_b[e],
                                    gla_norm_g[e], last)
        else:
            o = l // 2
            xc, xl = mixer_neighbourhood(xc, xl, mod_ctx[:3], mod_lat[:3], norm1_g[l], na_w_qkv[o],
                                         na_w_out[o], na_rpb[o], last)
        w1, w3, w2 = exp_w1[l].astype(BF16), exp_w3[l].astype(BF16), exp_w2[l].astype(BF16)
        xl = expert_choice_moe(xl, norm2_g[l], mod_lat[3], mod_lat[4], mod_lat[5], router_w[l], w1, w3, w2)
        if not last:
            xc = expert_choice_moe(xc, norm2_g[l], mod_ctx[3], mod_ctx[4], mod_ctx[5], router_w[l], w1, w3, w2)
    return rmsnorm_final(xl, final_g)
```

```python
import functools

import jax
import jax.numpy as jnp
import numpy as np
from jax import lax
from jax.experimental import pallas as pl
from jax.experimental.pallas import tpu as pltpu

F32 = jnp.float32
BF16 = jnp.bfloat16

GRID_W = 64
N_MOD = 6
EPS = 1e-6
GLA_HEADS = 4
GLA_RANK = 16
GLA_GATE_NORM = 16.0
GLA_CHUNK = 64
ROPE_BASE = 10000.0
FNET_GROUPS = 4
NA_HEADS = 16
NA_KR_MAX = 8
NA_KC = 16
N_EXPERTS = 16
EC_CAPACITY_FACTOR = 2

LANES = 128
VMEM_LIMIT_BYTES = 56 * 1024 * 1024
MASK_VALUE = -1e30

NA_ROWS_PER_BLOCK = 4
NA_HEADS_PER_STEP = 8


def _cparams(*sem):
    return pltpu.CompilerParams(dimension_semantics=sem, vmem_limit_bytes=VMEM_LIMIT_BYTES)


def _row_tile(t, cap):
    if t <= cap:
        return t
    for step in (LANES, 8):
        for tm in range(cap - cap % step, 0, -step):
            if t % tm == 0:
                return tm
    raise ValueError(f"no tile for {t} under {cap}")


def _ada_body(c_ref, w_ref, b_ref, o_ref):
    c = c_ref[...]
    a = (c * jax.nn.sigmoid(c)).astype(BF16)
    acc = jnp.dot(a, w_ref[0].astype(BF16), preferred_element_type=F32)
    o_ref[0] = acc + b_ref[0]


def ada_all_layers(cond, ada_w, ada_b):
    depth, d, n = ada_w.shape
    r = cond.shape[0]
    tn = _row_tile(n, 1024)
    return pl.pallas_call(
        _ada_body,
        grid=(depth, n // tn),
        in_specs=[
            pl.BlockSpec((r, d), lambda l, j: (0, 0)),
            pl.BlockSpec((1, d, tn), lambda l, j: (l, 0, j)),
            pl.BlockSpec((1, 1, tn), lambda l, j: (l, 0, j)),
        ],
        out_specs=pl.BlockSpec((1, r, tn), lambda l, j: (l, 0, j)),
        out_shape=jax.ShapeDtypeStruct((depth, r, n), F32),
        compiler_params=_cparams("parallel", "parallel"),
        name="ada_params",
    )(cond, ada_w, ada_b.reshape(depth, 1, n))


def _norm_mod(x, g, shift, scale):
    ms = jnp.mean(x * x, axis=-1, keepdims=True)
    y = x * lax.rsqrt(ms + EPS)
    return (y * g) * (1.0 + scale) + shift


def _nm_mm_body(x_ref, g_ref, sh_ref, sc_ref, w_ref, cs_ref, o_ref, h_scr):
    @pl.when(pl.program_id(2) == 0)
    def _():
        h = _norm_mod(x_ref[0], g_ref[...], sh_ref[0], sc_ref[0])
        h_scr[...] = h.astype(h_scr.dtype)

    acc = jnp.dot(h_scr[...], w_ref[...], preferred_element_type=F32)
    o_ref[0] = (acc * cs_ref[...]).astype(o_ref.dtype)


def _nm_mm_aux_body(x_ref, g_ref, sh_ref, sc_ref, w_ref, cs_ref, wa_ref, o_ref, oa_ref, h_scr):
    @pl.when(pl.program_id(2) == 0)
    def _():
        h = _norm_mod(x_ref[0], g_ref[...], sh_ref[0], sc_ref[0])
        h_scr[...] = h.astype(h_scr.dtype)
        oa_ref[0] = jnp.dot(h_scr[...], wa_ref[...], preferred_element_type=F32)

    acc = jnp.dot(h_scr[...], w_ref[...], preferred_element_type=F32)
    o_ref[0] = (acc * cs_ref[...]).astype(o_ref.dtype)


def norm_mod_matmul(x, g, shift, scale, w, col_scale, *, out_dtype=BF16, w_aux=None, tm=1024, tn=2048):
    b, t, d = x.shape
    n = w.shape[1]
    tm = _row_tile(t, tm)
    tn = _row_tile(n, tn)
    in_specs = [
        pl.BlockSpec((1, tm, d), lambda bi, i, j: (bi, i, 0)),
        pl.BlockSpec((1, d), lambda bi, i, j: (0, 0)),
        pl.BlockSpec((1, 1, d), lambda bi, i, j: (bi, 0, 0)),
        pl.BlockSpec((1, 1, d), lambda bi, i, j: (bi, 0, 0)),
        pl.BlockSpec((d, tn), lambda bi, i, j: (0, j)),
        pl.BlockSpec((1, tn), lambda bi, i, j: (0, j)),
    ]
    args = [x, g.reshape(1, d), shift, scale, w, col_scale.reshape(1, n)]
    out_specs = pl.BlockSpec((1, tm, tn), lambda bi, i, j: (bi, i, j))
    out_shape = jax.ShapeDtypeStruct((b, t, n), out_dtype)
    body = _nm_mm_body
    if w_aux is not None:
        na = w_aux.shape[1]
        in_specs.append(pl.BlockSpec((d, na), lambda bi, i, j: (0, 0)))
        args.append(w_aux)
        out_specs = (out_specs, pl.BlockSpec((1, tm, na), lambda bi, i, j: (bi, i, 0)))
        out_shape = (out_shape, jax.ShapeDtypeStruct((b, t, na), F32))
        body = _nm_mm_aux_body
    return pl.pallas_call(
        body,
        grid=(b, t // tm, n // tn),
        in_specs=in_specs,
        out_specs=out_specs,
        out_shape=out_shape,
        scratch_shapes=[pltpu.VMEM((tm, d), BF16)],
        compiler_params=_cparams("parallel", "parallel", "arbitrary"),
        name="norm_mod_matmul",
    )(*args)


def _nm_router_body(x_ref, g_ref, sh_ref, sc_ref, rwt_ref, h_ref, lg_ref):
    h = _norm_mod(x_ref[0], g_ref[...], sh_ref[0], sc_ref[0])
    bits = lax.bitcast_convert_type(h.astype(BF16).astype(F32), jnp.uint32)
    half = bits.shape[1] // 2
    h_ref[0] = lax.shift_right_logical(bits[:, :half], jnp.uint32(16)) | bits[:, half:]
    lg_ref[0] = lax.dot_general(rwt_ref[...], h, (((1,), (1,)), ((), ())),
                                preferred_element_type=F32, precision=lax.Precision.HIGHEST)


def norm_mod_router(x, g, shift, scale, router_w, *, tm=512):
    b, t, d = x.shape
    e = router_w.shape[1]
    tm = _row_tile(t, tm)
    assert d % (2 * LANES) == 0
    return pl.pallas_call(
        _nm_router_body,
        grid=(b, t // tm),
        in_specs=[
            pl.BlockSpec((1, tm, d), lambda bi, i: (bi, i, 0)),
            pl.BlockSpec((1, d), lambda bi, i: (0, 0)),
            pl.BlockSpec((1, 1, d), lambda bi, i: (bi, 0, 0)),
            pl.BlockSpec((1, 1, d), lambda bi, i: (bi, 0, 0)),
            pl.BlockSpec((e, d), lambda bi, i: (0, 0)),
        ],
        out_specs=(
            pl.BlockSpec((1, tm, d // 2), lambda bi, i: (bi, i, 0)),
            pl.BlockSpec((1, e, tm), lambda bi, i: (bi, 0, i)),
        ),
        out_shape=(jax.ShapeDtypeStruct((b, t, d // 2), jnp.uint32), jax.ShapeDtypeStruct((b, e, t), F32)),
        compiler_params=_cparams("parallel", "parallel"),
        name="norm_mod_router",
    )(x, g.reshape(1, d), shift, scale, router_w.T)


def _route_body(lg_ref, u_ref, lb_ref, ui_ref, pos_ref, gs_ref, idx_ref, rs_ref, *, cap, t_valid):
    _, e, r, _ = lg_ref.shape
    lg = lg_ref[0]
    mx = jnp.max(lg, axis=0, keepdims=True)
    ex = jnp.exp(lg - mx)
    aff = ex / jnp.sum(ex, axis=0, keepdims=True)
    if t_valid < r * LANES:
        tok = (lax.broadcasted_iota(jnp.int32, (e, r, LANES), 1) * LANES
               + lax.broadcasted_iota(jnp.int32, (e, r, LANES), 2))
        aff = jnp.where(tok < t_valid, aff, -1.0)
    bits = lax.bitcast_convert_type(aff, jnp.int32)

    def count(mask):
        s1 = jnp.sum(jnp.where(mask, 1.0, 0.0), axis=1, keepdims=True)
        return jnp.sum(s1, axis=2, keepdims=True)

    def search(i, prefix):
        cand = prefix | lax.shift_left(jnp.int32(1), 30 - i)
        return jnp.where(count(bits >= cand) >= cap, cand, prefix)

    thr = lax.fori_loop(0, 31, search, jnp.zeros((e, 1, 1), jnp.int32))

    def cumsum_incl(mask):
        x2 = jnp.where(mask, 1.0, 0.0).reshape(e * r, LANES).astype(BF16)
        within = jnp.dot(x2, u_ref[...], preferred_element_type=F32)
        rowtot = jnp.broadcast_to(within[:, LANES - 1:LANES], (e * r, LANES)).astype(BF16)
        rowoff = jnp.dot(lb_ref[...], rowtot, preferred_element_type=F32)
        return within.reshape(e, r, LANES), rowoff.reshape(e, r, LANES)

    gt = bits > thr
    eq = bits == thr
    need = cap - count(gt)
    w_eq, o_eq = cumsum_incl(eq)
    sel = gt | (eq & (w_eq + o_eq <= need))
    within, rowoff = cumsum_incl(sel)
    pos = within + rowoff - 1.0
    pos_ref[0] = jnp.where(sel, pos, -1.0).astype(jnp.int32)
    gs_ref[0] = jnp.where(sel, aff, 0.0)
    rs_ref[0] = rowoff.astype(jnp.int32)

    s_col = lax.broadcasted_iota(jnp.int32, (cap, 1), 0).astype(F32)
    ones8 = jnp.ones((8, LANES), BF16)
    lane_r = lax.broadcasted_iota(jnp.int32, (cap, r), 1).astype(F32)
    for ei in range(e):
        sel_e = jnp.where(sel[ei], 1.0, 0.0).astype(BF16)
        rowtot_l = lax.dot_general(ones8, sel_e, (((1,), (1,)), ((), ())),
                                   preferred_element_type=F32)
        rowend_l = jnp.dot(rowtot_l.astype(BF16), ui_ref[...], preferred_element_type=F32)
        before = rowend_l[0:1, :] <= s_col
        row_s = jnp.sum(jnp.where(before, 1.0, 0.0), axis=1, keepdims=True)
        off_s = jnp.sum(jnp.where(before, rowtot_l[0:1, :], 0.0), axis=1, keepdims=True)
        onehot = jnp.where(lane_r == row_s, 1.0, 0.0).astype(BF16)
        w_rows = jnp.dot(onehot, within[ei].astype(BF16), preferred_element_type=F32)
        j_s = jnp.sum(jnp.where(w_rows <= s_col - off_s, 1.0, 0.0), axis=1, keepdims=True)
        idx_ref[0, ei] = jnp.broadcast_to(row_s * LANES + j_s, (cap, LANES)).astype(jnp.int32)


def route(logits, cap):
    b, e, t = logits.shape
    r = max(-(-t // LANES), 8)
    tp = r * LANES
    lg = jnp.pad(logits, ((0, 0), (0, 0), (0, tp - t))).reshape(b, e, r, LANES)
    tri = np.arange(LANES)[:, None] <= np.arange(LANES)[None, :]
    u = jnp.asarray(tri, BF16)
    ii = np.arange(e * r)
    lb = jnp.asarray((ii[:, None] // r == ii[None, :] // r) & (ii[None, :] % r < ii[:, None] % r), BF16)
    ui = jnp.asarray(np.arange(r)[:, None] <= np.arange(r)[None, :], BF16)
    blk = pl.BlockSpec((1, e, r, LANES), lambda bi: (bi, 0, 0, 0))
    pos, gs, idx, rs = pl.pallas_call(
        functools.partial(_route_body, cap=cap, t_valid=t),
        grid=(b,),
        in_specs=[blk,
                  pl.BlockSpec((LANES, LANES), lambda bi: (0, 0)),
                  pl.BlockSpec((e * r, e * r), lambda bi: (0, 0)),
                  pl.BlockSpec((r, r), lambda bi: (0, 0))],
        out_specs=(blk, blk, pl.BlockSpec((1, e, cap, LANES), lambda bi: (bi, 0, 0, 0)), blk),
        out_shape=(jax.ShapeDtypeStruct((b, e, r, LANES), jnp.int32),
                   jax.ShapeDtypeStruct((b, e, r, LANES), F32),
                   jax.ShapeDtypeStruct((b, e, cap, LANES), jnp.int32),
                   jax.ShapeDtypeStruct((b, e, r, LANES), jnp.int32)),
        compiler_params=_cparams("parallel"),
        name="route",
    )(lg, u, lb, ui)
    return (pos.reshape(b, e, tp)[:, :, :t], gs.reshape(b, e, tp)[:, :, :t], idx[..., 0], rs[..., 0])


def _mm_res_body(a_ref, w_ref, r_ref, g_ref, o_ref):
    acc = jnp.dot(a_ref[0], w_ref[...], preferred_element_type=F32)
    o_ref[0] = r_ref[0] + g_ref[0] * acc


def matmul_gated_residual(a, w, res, gate, *, tm=512, tn=2048):
    b, t, k = a.shape
    n = w.shape[1]
    tm = _row_tile(t, tm)
    tn = _row_tile(n, tn)
    return pl.pallas_call(
        _mm_res_body,
        grid=(b, t // tm, n // tn),
        in_specs=[
            pl.BlockSpec((1, tm, k), lambda bi, i, j: (bi, i, 0)),
            pl.BlockSpec((k, tn), lambda bi, i, j: (0, j)),
            pl.BlockSpec((1, tm, tn), lambda bi, i, j: (bi, i, j)),
            pl.BlockSpec((1, 1, tn), lambda bi, i, j: (bi, 0, j)),
        ],
        out_specs=pl.BlockSpec((1, tm, tn), lambda bi, i, j: (bi, i, j)),
        out_shape=jax.ShapeDtypeStruct((b, t, n), F32),
        compiler_params=_cparams("parallel", "parallel", "parallel"),
        name="matmul_gated_residual",
    )(a, w, res, gate)


def _rmsnorm_body(x_ref, g_ref, o_ref):
    x = x_ref[0]
    ms = jnp.mean(x * x, axis=-1, keepdims=True)
    o_ref[0] = (x * lax.rsqrt(ms + EPS)) * g_ref[...]


def rmsnorm_final(x, g, *, tm=512):
    b, t, d = x.shape
    tm = _row_tile(t, tm)
    return pl.pallas_call(
        _rmsnorm_body,
        grid=(b, t // tm),
        in_specs=[pl.BlockSpec((1, tm, d), lambda bi, i: (bi, i, 0)),
                  pl.BlockSpec((1, d), lambda bi, i: (0, 0))],
        out_specs=pl.BlockSpec((1, tm, d), lambda bi, i: (bi, i, 0)),
        out_shape=jax.ShapeDtypeStruct((b, t, d), F32),
        compiler_params=_cparams("parallel", "parallel"),
        name="rmsnorm_final",
    )(x, g.reshape(1, d))


def _na_body(q_ref, k0_ref, k1_ref, k2_ref, v0_ref, v1_ref, v2_ref, kc_ref, vc_ref, bias_ref, o_ref,
             *, heads, hd, qb):
    dn = (((1,), (1,)), ((), ()))
    k_refs = (k0_ref, k1_ref, k2_ref)
    v_refs = (v0_ref, v1_ref, v2_ref)
    for g in range(heads):
        sl = slice(g * hd, (g + 1) * hd)
        q = q_ref[0, :, sl]
        s = [lax.dot_general(q, k_refs[i][0, :, sl], dn, preferred_element_type=F32)
             + bias_ref[0, g, :, i * qb:(i + 1) * qb] for i in range(3)]
        s.append(lax.dot_general(q, kc_ref[0, :, sl], dn, preferred_element_type=F32))
        m = functools.reduce(jnp.maximum, [jnp.max(si, axis=-1, keepdims=True) for si in s])
        p = [jnp.exp(si - m) for si in s]
        l = functools.reduce(jnp.add, [jnp.sum(pi, axis=-1, keepdims=True) for pi in p])
        vals = [v_refs[i][0, :, sl] for i in range(3)] + [vc_ref[0, :, sl]]
        o = functools.reduce(jnp.add, [jnp.dot(pi.astype(BF16), vi, preferred_element_type=F32)
                                       for pi, vi in zip(p, vals)])
        o_ref[0, :, sl] = (o / l).astype(o_ref.dtype)


def _na_bias_table(rpb, rows):
    r_blk = NA_ROWS_PER_BLOCK
    nblk = rows // r_blk
    kr = min(NA_KR_MAX, rows)
    tables = []
    for rb in (0, 1, nblk - 1):
        ks = min(max(rb - 1, 0), nblk - 3) * r_blk
        q_row = rb * r_blk + np.arange(r_blk)
        rs = np.clip(q_row - kr // 2, 0, rows - kr)
        k_row = ks + np.arange(3 * r_blk)
        ok_row = (k_row[None, :] >= rs[:, None]) & (k_row[None, :] < rs[:, None] + kr)
        rel_row = np.clip(k_row[None, :] - q_row[:, None] + (NA_KR_MAX - 1), 0, 2 * NA_KR_MAX - 2)
        q_col = np.arange(GRID_W)
        cs = np.clip(q_col - NA_KC // 2, 0, GRID_W - NA_KC)
        k_col = np.arange(GRID_W)
        ok_col = (k_col[None, :] >= cs[:, None]) & (k_col[None, :] < cs[:, None] + NA_KC)
        rel_col = np.clip(k_col[None, :] - q_col[:, None] + (NA_KC - 1), 0, 2 * NA_KC - 2)
        hi = lax.Precision.HIGHEST
        oh_r = jnp.asarray(rel_row[:, :, None] == np.arange(2 * NA_KR_MAX - 1), F32)
        oh_c = jnp.asarray(np.arange(2 * NA_KC - 1)[:, None, None] == rel_col[None], F32)
        t = jnp.einsum('qkr,hrc->hqkc', oh_r, rpb, precision=hi)
        t = jnp.einsum('hqkc,cwv->hqwkv', t, oh_c, precision=hi)
        ok = ok_row[:, None, :, None] & ok_col[None, :, None, :]
        t = jnp.where(ok[None], t, MASK_VALUE)
        tables.append(t.reshape(rpb.shape[0], r_blk * GRID_W, 3 * r_blk * GRID_W))
    return jnp.stack(tables)


def neighbourhood_attention(qkv, qkv_ctx, rpb):
    b, t, d3 = qkv.shape
    d = d3 // 3
    l = qkv_ctx.shape[1]
    hd = d // NA_HEADS
    rows = t // GRID_W
    r_blk = NA_ROWS_PER_BLOCK
    qb = r_blk * GRID_W
    nblk = rows // r_blk
    assert rows % r_blk == 0 and nblk >= 3 and min(NA_KR_MAX, rows) + r_blk <= 3 * r_blk
    g = NA_HEADS_PER_STEP
    gw = g * hd
    nhg = NA_HEADS // g
    bias = _na_bias_table(rpb.astype(F32), rows)

    def kstart(rb):
        return jnp.clip(rb - 1, 0, nblk - 3)

    def pattern(rb):
        return jnp.where(rb == 0, 0, jnp.where(rb == nblk - 1, 2, 1))

    q_spec = pl.BlockSpec((1, qb, gw), lambda hg, bi, rb: (bi, rb, hg))
    k_specs = [pl.BlockSpec((1, qb, gw), functools.partial(
        lambda hg, bi, rb, i: (bi, kstart(rb) + i, nhg + hg), i=i)) for i in range(3)]
    v_specs = [pl.BlockSpec((1, qb, gw), functools.partial(
        lambda hg, bi, rb, i: (bi, kstart(rb) + i, 2 * nhg + hg), i=i)) for i in range(3)]
    kc_spec = pl.BlockSpec((1, l, gw), lambda hg, bi, rb: (bi, 0, nhg + hg))
    vc_spec = pl.BlockSpec((1, l, gw), lambda hg, bi, rb: (bi, 0, 2 * nhg + hg))
    bias_spec = pl.BlockSpec((1, g, qb, 3 * qb), lambda hg, bi, rb: (pattern(rb), hg, 0, 0))
    return pl.pallas_call(
        functools.partial(_na_body, heads=g, hd=hd, qb=qb),
        grid=(nhg, b, nblk),
        in_specs=[q_spec] + k_specs + v_specs + [kc_spec, vc_spec, bias_spec],
        out_specs=pl.BlockSpec((1, qb, gw), lambda hg, bi, rb: (bi, rb, hg)),
        out_shape=jax.ShapeDtypeStruct((b, t, d), BF16),
        compiler_params=_cparams("parallel", "parallel", "arbitrary"),
        name="neighbourhood_attention",
    )(qkv, qkv, qkv, qkv, qkv, qkv, qkv, qkv_ctx, qkv_ctx, bias)


MOE_GATHER_UNROLL = 8
MOE_ISSUE_GROUPS = 8


def _moe_ffn_body(first_ref, next_ref, h_hbm, w1_ref, w3_ref, w2_ref, o_ref, xbuf, xb_scr, sem):
    i = pl.program_id(1)
    n_tiles = pl.num_programs(1)
    step = pl.program_id(0) * n_tiles + i
    tm = xbuf.shape[1]
    slot = lax.rem(step, 2)

    def row_copy(src_row, sl, r):
        return pltpu.make_async_copy(h_hbm.at[pl.ds(src_row, 1)], xbuf.at[sl, pl.ds(r, 1)], sem.at[sl])

    @pl.when(step == 0)
    def _():
        def body(r, carry):
            row_copy(first_ref[0, 0, r], slot, r).start()
            return carry
        lax.fori_loop(0, tm, body, 0, unroll=MOE_GATHER_UNROLL)

    pltpu.make_async_copy(h_hbm.at[pl.ds(0, tm)], xbuf.at[slot], sem.at[slot]).wait()
    packed = xbuf[slot]
    half = packed.shape[1]
    xb_scr[:, :half] = lax.bitcast_convert_type(lax.shift_left(packed, jnp.uint32(16)), F32).astype(BF16)
    xb_scr[:, half:] = lax.bitcast_convert_type(packed & jnp.uint32(0xFFFF0000), F32).astype(BF16)

    per_group = tm // MOE_ISSUE_GROUPS

    def issue_group(gi):
        for r in range(gi * per_group, (gi + 1) * per_group):
            row_copy(next_ref[0, 0, i * tm + r], 1 - slot, r).start()

    n_chunks = MOE_ISSUE_GROUPS // 2
    fc = w1_ref.shape[3] // n_chunks
    parts = []
    for c in range(n_chunks):
        cols = slice(c * fc, (c + 1) * fc)
        issue_group(2 * c)
        a = jnp.dot(xb_scr[...], w1_ref[0, 0, :, cols], preferred_element_type=F32)
        issue_group(2 * c + 1)
        bb = jnp.dot(xb_scr[...], w3_ref[0, 0, :, cols], preferred_element_type=F32)
        parts.append((a * jax.nn.sigmoid(a) * bb).astype(BF16))
    hm = jnp.concatenate(parts, axis=1)
    o_ref[0] = jnp.dot(hm, w2_ref[0, 0], preferred_element_type=F32).astype(o_ref.dtype)

    @pl.when(step == pl.num_programs(0) * n_tiles - 1)
    def _():
        pltpu.make_async_copy(h_hbm.at[pl.ds(0, tm)], xbuf.at[1 - slot], sem.at[1 - slot]).wait()


def moe_ffn(h, rows, w1, w3, w2, layer, *, tm=512):
    e, m = rows.shape
    d = 2 * h.shape[1]
    f = w1.shape[3]
    tm = _row_tile(m, tm)
    assert tm % MOE_ISSUE_GROUPS == 0 and f % (MOE_ISSUE_GROUPS // 2 * LANES) == 0
    rows_next = jnp.roll(rows.reshape(-1), -tm).reshape(e, 1, m)
    rows_first = rows[0, :tm].reshape(1, 1, tm)
    return pl.pallas_call(
        _moe_ffn_body,
        grid=(e, m // tm),
        in_specs=[
            pl.BlockSpec((1, 1, tm), lambda ei, i: (0, 0, 0), memory_space=pltpu.SMEM),
            pl.BlockSpec((1, 1, m), lambda ei, i: (ei, 0, 0), memory_space=pltpu.SMEM),
            pl.BlockSpec(memory_space=pl.ANY),
            pl.BlockSpec((1, 1, d, f), lambda ei, i: (layer, ei, 0, 0)),
            pl.BlockSpec((1, 1, d, f), lambda ei, i: (layer, ei, 0, 0)),
            pl.BlockSpec((1, 1, f, d), lambda ei, i: (layer, ei, 0, 0)),
        ],
        out_specs=pl.BlockSpec((1, tm, d), lambda ei, i: (ei, i, 0)),
        out_shape=jax.ShapeDtypeStruct((e, m, d), BF16),
        scratch_shapes=[pltpu.VMEM((2, tm, d // 2), jnp.uint32), pltpu.VMEM((tm, d), BF16),
                        pltpu.SemaphoreType.DMA((2,))],
        compiler_params=_cparams("arbitrary", "arbitrary"),
        name="moe_ffn",
    )(rows_first, rows_next, h, w1, w3, w2)


COMBINE_TOKENS = 256
COMBINE_WINDOW = 64
BF16_SUBLANES = 16


def _combine_body(ts_ref, pos_ref, gs_ref, x_ref, g2_ref, ye_hbm, o_ref, ybuf, yextra, sem, sem_x,
                  *, n_exp, cap, win, n_batch):
    bi = pl.program_id(0)
    ti = pl.program_id(1)
    nt = pl.num_programs(1)
    step = bi * nt + ti
    total = n_batch * nt
    slot = lax.rem(step, 2)
    tt = pos_ref.shape[2]

    def tile_start(b, t, e):
        return ts_ref[(b * (nt + 1) + t) * n_exp + e]

    def window_start(b, t, e, w):
        st = tile_start(b, t, e)
        a = lax.shift_left(lax.shift_right_logical(st, 4), 4) + w * win
        return jnp.minimum(a, cap - win)

    def window_copy(b, e, a, dst, dsem):
        row0 = pl.multiple_of((e * n_batch + b) * cap + a, BF16_SUBLANES)
        return pltpu.make_async_copy(ye_hbm.at[pl.ds(row0, win)], dst, dsem)

    def issue(b, t, sl):
        for e in range(n_exp):
            window_copy(b, e, window_start(b, t, e, 0), ybuf.at[sl, e], sem.at[sl]).start()

    @pl.when(step == 0)
    def _():
        issue(bi, ti, slot)

    for e in range(n_exp):
        window_copy(bi, e, 0, ybuf.at[slot, e], sem.at[slot]).wait()

    @pl.when(step + 1 < total)
    def _():
        nxt = step + 1
        issue(nxt // nt, lax.rem(nxt, nt), 1 - slot)

    m_iota = lax.broadcasted_iota(jnp.int32, (win, tt), 0)
    dn = (((0,), (0,)), ((), ()))

    def gather_matrix(w):
        rows = []
        for e in range(n_exp):
            a = window_start(bi, ti, e, w)
            lo = window_start(bi, ti, e, 0) + w * win
            p = pos_ref[0, e:e + 1, :]
            hit = (p - a == m_iota) & (p >= lo)
            rows.append(jnp.where(hit, gs_ref[0, e:e + 1, :], 0.0).astype(BF16))
        return jnp.concatenate(rows, axis=0)

    d = ybuf.shape[-1]
    acc = lax.dot_general(gather_matrix(0), ybuf[slot].reshape(n_exp * win, d), dn,
                          preferred_element_type=F32)

    n_win = jnp.int32(1)
    for e in range(n_exp):
        span = tile_start(bi, ti + 1, e) - window_start(bi, ti, e, 0)
        n_win = jnp.maximum(n_win, (span + win - 1) // win)

    def extra(w, acc):
        for e in range(n_exp):
            cp = window_copy(bi, e, window_start(bi, ti, e, w), yextra.at[e], sem_x)
            cp.start()
            cp.wait()
        return acc + lax.dot_general(gather_matrix(w), yextra[...].reshape(n_exp * win, d), dn,
                                     preferred_element_type=F32)

    acc = lax.fori_loop(1, n_win, extra, acc)
    o_ref[0] = x_ref[0] + g2_ref[0] * acc


def moe_combine(x, g2, ye, pos, gs, tile_start, cap):
    b, t, d = x.shape
    e = pos.shape[1]
    tt = min(COMBINE_TOKENS, t)
    win = min(COMBINE_WINDOW, cap)
    nt = t // tt
    grid_spec = pltpu.PrefetchScalarGridSpec(
        num_scalar_prefetch=1,
        grid=(b, nt),
        in_specs=[
            pl.BlockSpec((1, e, tt), lambda bi, ti, ts: (bi, 0, ti)),
            pl.BlockSpec((1, e, tt), lambda bi, ti, ts: (bi, 0, ti)),
            pl.BlockSpec((1, tt, d), lambda bi, ti, ts: (bi, ti, 0)),
            pl.BlockSpec((1, 1, d), lambda bi, ti, ts: (bi, 0, 0)),
            pl.BlockSpec(memory_space=pl.ANY),
        ],
        out_specs=pl.BlockSpec((1, tt, d), lambda bi, ti, ts: (bi, ti, 0)),
        scratch_shapes=[
            pltpu.VMEM((2, e, win, d), BF16),
            pltpu.VMEM((e, win, d), BF16),
            pltpu.SemaphoreType.DMA((2,)),
            pltpu.SemaphoreType.DMA(()),
        ],
    )
    return pl.pallas_call(
        functools.partial(_combine_body, n_exp=e, cap=cap, win=win, n_batch=b),
        grid_spec=grid_spec,
        out_shape=jax.ShapeDtypeStruct((b, t, d), F32),
        compiler_params=_cparams("arbitrary", "arbitrary"),
        name="moe_combine",
    )(tile_start, pos, gs, x, g2, ye)


def expert_choice_moe(x, g, shift, scale, g2, router_w, w1, w3, w2, layer):
    b, t, d = x.shape
    e = router_w.shape[1]
    cap = EC_CAPACITY_FACTOR * t // e
    h, logits = norm_mod_router(x, g, shift, scale, router_w)
    pos, gs, idx, rowstart = route(logits, cap)
    rows = idx + (jnp.arange(b, dtype=jnp.int32) * t)[:, None, None]
    rows = jnp.swapaxes(rows, 0, 1).reshape(e, b * cap)
    ye = moe_ffn(h.reshape(b * t, d // 2), rows, w1, w3, w2, layer).reshape(e * b * cap, d)
    tt = min(COMBINE_TOKENS, t)
    ts = rowstart[:, :, ::tt // LANES][:, :, :t // tt]
    ts = jnp.concatenate([jnp.swapaxes(ts, 1, 2), jnp.full((b, 1, e), cap, jnp.int32)], axis=1)
    return moe_combine(x, g2, ye, pos, gs, ts.reshape(-1), cap)


def _axial_rope_angles(n_tok, head_dim):
    n_freq = head_dim // 4
    inv = ROPE_BASE ** (-jnp.arange(n_freq, dtype=F32) / n_freq)
    tt = jnp.arange(n_tok)
    row = (tt // GRID_W).astype(F32)
    col = (tt % GRID_W).astype(F32)
    return row[:, None] * inv[None], col[:, None] * inv[None]


GLA_BLOCK = 4 * GLA_CHUNK
GLA_HEADS_PER_STEP = 4


def _log_sigmoid(x):
    return jnp.minimum(x, 0.0) - jnp.log(1.0 + jnp.exp(-jnp.abs(x)))


def _gla_block(q_ref, k_ref, v_ref, lr_ref, cos_ref, sin_ref, a_ref, ab_ref, tri_ref, perm_ref, st_ref, o_ref,
               *, reverse, g, hk, hv):
    ks = slice(g * hk, (g + 1) * hk)
    vs = slice(g * hv, (g + 1) * hv)
    nt = (((1,), (1,)), ((), ()))
    tn = (((0,), (0,)), ((), ()))
    hi = lax.Precision.HIGHEST
    n = GLA_BLOCK
    c = GLA_CHUNK
    pre = jnp.dot(lr_ref[0], a_ref[0, :, ks], preferred_element_type=F32, precision=hi) + ab_ref[0, :, ks]
    la = _log_sigmoid(pre) / GLA_GATE_NORM
    la_hi = la.astype(BF16)
    la_r = la - la_hi.astype(F32)
    la_mid = la_r.astype(BF16)
    la_lo = (la_r - la_mid.astype(F32)).astype(BF16)
    tri = tri_ref[...]
    bcum = (jnp.dot(tri, la_hi, preferred_element_type=F32) + jnp.dot(tri, la_mid, preferred_element_type=F32)
            + jnp.dot(tri, la_lo, preferred_element_type=F32))
    q = q_ref[0, :, ks]
    k = k_ref[0, :, ks]
    v = v_ref[0, :, vs]
    qf = q.astype(F32)
    kf = k.astype(F32)
    if cos_ref is not None:
        cs, sn = cos_ref[...], sin_ref[...]
        qf = qf * cs + jnp.dot(q, perm_ref[...], preferred_element_type=F32) * sn
        kf = kf * cs + jnp.dot(k, perm_ref[...], preferred_element_type=F32) * sn
    last_row = [bcum[(i * c if reverse else i * c + c - 1):(i * c + 1 if reverse else i * c + c)]
                for i in range(n // c)]
    blast = jnp.concatenate([jnp.broadcast_to(r, (c, r.shape[1])) for r in last_row], axis=0)
    qe = (qf * jnp.exp(bcum)).astype(BF16)
    ke = (kf * jnp.exp(-bcum)).astype(BF16)
    kd = (kf * jnp.exp(blast - bcum)).astype(BF16)
    row = lax.broadcasted_iota(jnp.int32, (n, n), 0)
    col = lax.broadcasted_iota(jnp.int32, (n, n), 1)
    same_chunk = (row // c) == (col // c)
    causal = (col >= row) if reverse else (col <= row)
    att = jnp.where(same_chunk & causal, lax.dot_general(qe, ke, nt, preferred_element_type=F32), 0.0)
    o_intra = jnp.dot(att.astype(BF16), v, preferred_element_type=F32)
    order = range(n // c - 1, -1, -1) if reverse else range(n // c)
    for i in order:
        sl = slice(i * c, (i + 1) * c)
        st = st_ref[g]
        o_ref[0, 0, sl, vs] = o_intra[sl] + lax.dot_general(qe[sl], st.astype(BF16), nt,
                                                            preferred_element_type=F32)
        u_t = lax.dot_general(v[sl], kd[sl], tn, preferred_element_type=F32)
        st_ref[g] = st * jnp.exp(last_row[i]) + u_t


def _gla_body(qc_ref, kc_ref, vc_ref, lrc_ref, ql_ref, kl_ref, vl_ref, lrl_ref, cos_ref, sin_ref,
              a_ref, ab_ref, tri_ref, perm_ref, oc_ref, ol_ref, st_ref, *, reverse, heads, hk, hv):
    j = pl.program_id(2)

    @pl.when(j == 0)
    def _():
        st_ref[...] = jnp.zeros_like(st_ref)
        for g in range(heads):
            _gla_block(qc_ref, kc_ref, vc_ref, lrc_ref, None, None, a_ref, ab_ref, tri_ref, perm_ref, st_ref,
                       oc_ref, reverse=reverse, g=g, hk=hk, hv=hv)

    @pl.when(j > 0)
    def _():
        for g in range(heads):
            _gla_block(ql_ref, kl_ref, vl_ref, lrl_ref, cos_ref, sin_ref, a_ref, ab_ref, tri_ref, perm_ref,
                       st_ref, ol_ref, reverse=reverse, g=g, hk=hk, hv=hv)


def gla_scan(p_c, lr_c, p_l, lr_l, a_up, a_b, cos, sin, *, dk, dv, reverse):
    b, tc, _ = p_c.shape
    tl = p_l.shape[1]
    hk, hv = dk // GLA_HEADS, dv // GLA_HEADS
    n = GLA_BLOCK
    assert tc == n and tl % n == 0 and hk == LANES
    nl = tl // n
    hps = GLA_HEADS_PER_STEP
    kw, vw = hps * hk, hps * hv
    qo, ko, vo = 0, dk // kw, 2 * dk // vw
    ii = np.arange(n)
    same = ii[:, None] // GLA_CHUNK == ii[None, :] // GLA_CHUNK
    tri = jnp.asarray(same & ((ii[None, :] >= ii[:, None]) if reverse else (ii[None, :] <= ii[:, None])), BF16)
    jj = np.arange(hk)
    partner = np.where((jj % (hk // 2)) < hk // 4, jj + hk // 4, jj - hk // 4)
    perm = np.zeros((hk, hk), np.float32)
    perm[partner, jj] = 1.0
    perm = jnp.asarray(perm, BF16)

    def lat_blk(j):
        blk = jnp.clip(j - 1, 0, nl - 1)
        return (nl - 1 - blk) if reverse else blk

    ctx_spec = lambda width, off: pl.BlockSpec((1, n, width), lambda bi, h, j: (bi, 0, off + h))
    lat_spec = lambda width, off: pl.BlockSpec((1, n, width), lambda bi, h, j: (bi, lat_blk(j), off + h))
    in_specs = [
        ctx_spec(kw, qo), ctx_spec(kw, ko), ctx_spec(vw, vo),
        pl.BlockSpec((1, n, LANES), lambda bi, h, j: (bi, 0, 0)),
        lat_spec(kw, qo), lat_spec(kw, ko), lat_spec(vw, vo),
        pl.BlockSpec((1, n, LANES), lambda bi, h, j: (bi, lat_blk(j), 0)),
        pl.BlockSpec((n, hk), lambda bi, h, j: (lat_blk(j), 0)),
        pl.BlockSpec((n, hk), lambda bi, h, j: (lat_blk(j), 0)),
        pl.BlockSpec((1, LANES, kw), lambda bi, h, j: (0, 0, h)),
        pl.BlockSpec((1, 1, kw), lambda bi, h, j: (0, 0, h)),
        pl.BlockSpec((n, n), lambda bi, h, j: (0, 0)),
        pl.BlockSpec((hk, hk), lambda bi, h, j: (0, 0)),
    ]
    out_specs = (
        pl.BlockSpec((1, 1, n, vw), lambda bi, h, j: (0, bi, 0, h)),
        pl.BlockSpec((1, 1, n, vw), lambda bi, h, j: (0, bi, lat_blk(j), h)),
    )
    o_c, o_l = pl.pallas_call(
        functools.partial(_gla_body, reverse=reverse, heads=hps, hk=hk, hv=hv),
        grid=(b, GLA_HEADS // hps, nl + 1),
        in_specs=in_specs,
        out_specs=out_specs,
        out_shape=(jax.ShapeDtypeStruct((1, b, tc, dv), F32), jax.ShapeDtypeStruct((1, b, tl, dv), F32)),
        scratch_shapes=[pltpu.VMEM((hps, hv, hk), F32)],
        compiler_params=_cparams("parallel", "parallel", "arbitrary"),
        name="gla_scan_bwd" if reverse else "gla_scan_fwd",
    )(p_c, p_c, p_c, lr_c, p_l, p_l, p_l, lr_l, cos, sin, a_up[None], a_b[None], tri, perm)
    return o_c[0], o_l[0]


def _gla_out_body(of_ref, ob_ref, g_ref, ng_ref, y_ref, *, heads):
    o = of_ref[0] + ob_ref[0]
    hv = o.shape[1] // heads
    outs = []
    for h in range(heads):
        oh = o[:, h * hv:(h + 1) * hv]
        outs.append(oh * lax.rsqrt(jnp.mean(oh * oh, axis=-1, keepdims=True) + EPS))
    on = jnp.concatenate(outs, axis=1) * ng_ref[...]
    gg = g_ref[0].astype(F32)
    y_ref[0] = (on * (gg * jax.nn.sigmoid(gg))).astype(y_ref.dtype)


def gla_output(o_f, o_b, p, norm_g, *, g_off, tm=256):
    b, t, dv = o_f.shape
    tm = _row_tile(t, tm)
    blk = pl.BlockSpec((1, tm, dv), lambda bi, i: (bi, i, 0))
    return pl.pallas_call(
        functools.partial(_gla_out_body, heads=GLA_HEADS),
        grid=(b, t // tm),
        in_specs=[blk, blk, pl.BlockSpec((1, tm, dv), lambda bi, i: (bi, i, g_off // dv)),
                  pl.BlockSpec((1, dv), lambda bi, i: (0, 0))],
        out_specs=blk,
        out_shape=jax.ShapeDtypeStruct((b, t, dv), BF16),
        compiler_params=_cparams("parallel", "parallel"),
        name="gla_output",
    )(o_f, o_b, p, norm_g.reshape(1, dv))


FFT_T2 = 128
FFT_ROWS = 8


def _dft_mats(n):
    ang = 2.0 * np.pi * ((np.arange(n)[:, None] * np.arange(n)[None, :]) % n) / n
    return np.cos(ang), np.sin(ang)


def _chan_dft_body(u_ref, w_ref, z_ref):
    z_ref[0] = jnp.dot(u_ref[0], w_ref[...], preferred_element_type=F32)


def channel_dft(p, *, u_off, fd, tm=512):
    b, t, _ = p.shape
    gd = fd // FNET_GROUPS
    c, s = _dft_mats(gd)
    w = jnp.asarray(np.concatenate([c, -s], axis=1), BF16)
    tm = _row_tile(t, tm)
    return pl.pallas_call(
        _chan_dft_body,
        grid=(b, t // tm, FNET_GROUPS),
        in_specs=[pl.BlockSpec((1, tm, gd), lambda bi, i, g: (bi, i, u_off // gd + g)),
                  pl.BlockSpec((gd, 2 * gd), lambda bi, i, g: (0, 0))],
        out_specs=pl.BlockSpec((1, tm, 2 * gd), lambda bi, i, g: (bi, i, g)),
        out_shape=jax.ShapeDtypeStruct((b, t, 2 * fd), F32),
        compiler_params=_cparams("parallel", "parallel", "parallel"),
        name="channel_dft",
    )(p, w)


def _dft_direct_body(z_ref, f_ref, y_ref, *, gd, scale):
    z = z_ref[0]
    zz = jnp.concatenate([z[:, :gd], z[:, gd:]], axis=0).astype(BF16)
    y_ref[0] = jnp.dot(f_ref[...], zz, preferred_element_type=F32) * scale


def token_dft_direct(z, gd):
    b, t, w2 = z.shape
    c, s = _dft_mats(t)
    f = jnp.asarray(np.concatenate([c, s], axis=1), BF16)
    return pl.pallas_call(
        functools.partial(_dft_direct_body, gd=gd, scale=float((t * gd) ** -0.5)),
        grid=(b, FNET_GROUPS),
        in_specs=[pl.BlockSpec((1, t, 2 * gd), lambda bi, g: (bi, 0, g)),
                  pl.BlockSpec((t, 2 * t), lambda bi, g: (0, 0))],
        out_specs=pl.BlockSpec((1, t, gd), lambda bi, g: (bi, 0, g)),
        out_shape=jax.ShapeDtypeStruct((b, t, w2 // 2), F32),
        compiler_params=_cparams("parallel", "parallel"),
        name="token_dft_direct",
    )(z, f)


def _dft_stage1_body(z_ref, f_ref, twc_ref, tws_ref, o_ref, *, gd):
    for j in range(FFT_ROWS):
        z = z_ref[0, :, j, :]
        zz = jnp.concatenate([z[:, :gd], z[:, gd:]], axis=0).astype(BF16)
        a = jnp.dot(f_ref[...], zz, preferred_element_type=F32)
        t1 = a.shape[0] // 2
        ar, ai = a[:t1], a[t1:]
        tc = jnp.concatenate([twc_ref[j]] * (gd // LANES), axis=1)
        ts = jnp.concatenate([tws_ref[j]] * (gd // LANES), axis=1)
        o_ref[0, 0, :, j, :gd] = ar * tc + ai * ts
        o_ref[0, 0, :, j, gd:] = ai * tc - ar * ts


def _dft_stage2_body(b_ref, f_ref, y_ref, *, gd, scale):
    for j in range(FFT_ROWS):
        s = b_ref[0, 0, j]
        ss = jnp.concatenate([s[:, :gd], s[:, gd:]], axis=0).astype(BF16)
        y_ref[0, :, j, :] = jnp.dot(f_ref[...], ss, preferred_element_type=F32) * scale


def token_dft_two_stage(z, gd):
    b, t, w2 = z.shape
    t2 = FFT_T2
    t1 = t // t2
    assert t1 * t2 == t and t1 % FFT_ROWS == 0 and t2 % FFT_ROWS == 0 and gd % LANES == 0
    c1, s1 = _dft_mats(t1)
    f1 = jnp.asarray(np.block([[c1, s1], [-s1, c1]]), BF16)
    ang = 2.0 * np.pi * (np.arange(t2)[:, None] * np.arange(t1)[None, :]) / t
    twc = jnp.asarray(np.broadcast_to(np.cos(ang)[:, :, None], (t2, t1, LANES)), F32)
    tws = jnp.asarray(np.broadcast_to(np.sin(ang)[:, :, None], (t2, t1, LANES)), F32)
    c2, s2 = _dft_mats(t2)
    f2 = jnp.asarray(np.concatenate([c2, s2], axis=1), BF16)
    z4 = z.reshape(b, t1, t2, w2)
    bmid = pl.pallas_call(
        functools.partial(_dft_stage1_body, gd=gd),
        grid=(b, FNET_GROUPS, t2 // FFT_ROWS),
        in_specs=[pl.BlockSpec((1, t1, FFT_ROWS, 2 * gd), lambda bi, g, i: (bi, 0, i, g)),
                  pl.BlockSpec((2 * t1, 2 * t1), lambda bi, g, i: (0, 0)),
                  pl.BlockSpec((FFT_ROWS, t1, LANES), lambda bi, g, i: (i, 0, 0)),
                  pl.BlockSpec((FFT_ROWS, t1, LANES), lambda bi, g, i: (i, 0, 0))],
        out_specs=pl.BlockSpec((1, 1, t1, FFT_ROWS, 2 * gd), lambda bi, g, i: (bi, g, 0, i, 0)),
        out_shape=jax.ShapeDtypeStruct((b, FNET_GROUPS, t1, t2, 2 * gd), F32),
        compiler_params=_cparams("parallel", "parallel", "parallel"),
        name="token_dft_stage1",
    )(z4, f1, twc, tws)
    y = pl.pallas_call(
        functools.partial(_dft_stage2_body, gd=gd, scale=float((t * gd) ** -0.5)),
        grid=(b, FNET_GROUPS, t1 // FFT_ROWS),
        in_specs=[pl.BlockSpec((1, 1, FFT_ROWS, t2, 2 * gd), lambda bi, g, i: (bi, g, i, 0, 0)),
                  pl.BlockSpec((t2, 2 * t2), lambda bi, g, i: (0, 0))],
        out_specs=pl.BlockSpec((1, t2, FFT_ROWS, gd), lambda bi, g, i: (bi, 0, i, g)),
        out_shape=jax.ShapeDtypeStruct((b, t2, t1, w2 // 2), F32),
        compiler_params=_cparams("parallel", "parallel", "parallel"),
        name="token_dft_stage2",
    )(bmid, f2)
    return y.reshape(b, t, w2 // 2)


def fourier_mix(p, *, u_off, fd):
    gd = fd // FNET_GROUPS
    z = channel_dft(p, u_off=u_off, fd=fd)
    if z.shape[1] <= 512:
        return token_dft_direct(z, gd)
    return token_dft_two_stage(z, gd)


def _mm2_res_body(a1_ref, a2_ref, w1_ref, w2_ref, r_ref, g_ref, o_ref):
    acc = jnp.dot(a1_ref[0].astype(BF16), w1_ref[...], preferred_element_type=F32)
    acc = acc + jnp.dot(a2_ref[0].astype(BF16), w2_ref[...], preferred_element_type=F32)
    o_ref[0] = r_ref[0] + g_ref[0] * acc


def matmul2_gated_residual(a1, a2, w, res, gate, *, tm=512, tn=2048):
    b, t, k1 = a1.shape
    k2 = a2.shape[2]
    n = w.shape[1]
    tm = _row_tile(t, tm)
    tn = _row_tile(n, tn)
    return pl.pallas_call(
        _mm2_res_body,
        grid=(b, t // tm, n // tn),
        in_specs=[
            pl.BlockSpec((1, tm, k1), lambda bi, i, j: (bi, i, 0)),
            pl.BlockSpec((1, tm, k2), lambda bi, i, j: (bi, i, 0)),
            pl.BlockSpec((k1, tn), lambda bi, i, j: (0, j)),
            pl.BlockSpec((k2, tn), lambda bi, i, j: (k1 // k2, j)),
            pl.BlockSpec((1, tm, tn), lambda bi, i, j: (bi, i, j)),
            pl.BlockSpec((1, 1, tn), lambda bi, i, j: (bi, 0, j)),
        ],
        out_specs=pl.BlockSpec((1, tm, tn), lambda bi, i, j: (bi, i, j)),
        out_shape=jax.ShapeDtypeStruct((b, t, n), F32),
        compiler_params=_cparams("parallel", "parallel", "parallel"),
        name="matmul2_gated_residual",
    )(a1, a2, w, w, res, gate)


def mixer_gla_fnet(x_ctx, x_lat, mod_ctx, mod_lat, norm_g1, w_in, w_out,
                   a_up_f, a_b_f, a_up_b, a_b_b, gla_norm_g, last):
    dv = gla_norm_g.shape[0]
    dk = a_up_f.shape[1]
    t = x_lat.shape[1]
    hk = dk // GLA_HEADS
    n_main = 2 * dk + 2 * dv
    fd = w_in.shape[1] - n_main - 2 * GLA_RANK
    w_main = jnp.concatenate([w_in[:, :n_main], w_in[:, n_main + 2 * GLA_RANK:]], axis=1).astype(BF16)
    w_lr = jnp.pad(w_in[:, n_main:n_main + 2 * GLA_RANK], ((0, 0), (0, LANES - 2 * GLA_RANK))).astype(BF16)
    cs = jnp.concatenate([jnp.full((dk,), hk ** -0.5, F32), jnp.ones((w_main.shape[1] - dk,), F32)])
    p_l, lr_l = norm_mod_matmul(x_lat, norm_g1, mod_lat[0], mod_lat[1], w_main, cs, w_aux=w_lr)
    p_c, lr_c = norm_mod_matmul(x_ctx, norm_g1, mod_ctx[0], mod_ctx[1], w_main, cs, w_aux=w_lr)
    ang_row, ang_col = _axial_rope_angles(t, hk)
    cos = jnp.concatenate([jnp.cos(ang_row)] * 2 + [jnp.cos(ang_col)] * 2, axis=1)
    sin = jnp.concatenate([-jnp.sin(ang_row), jnp.sin(ang_row), -jnp.sin(ang_col), jnp.sin(ang_col)], axis=1)
    pad_f = ((0, LANES - GLA_RANK), (0, 0))
    pad_b = ((GLA_RANK, LANES - 2 * GLA_RANK), (0, 0))
    ofc, ofl = gla_scan(p_c, lr_c, p_l, lr_l, jnp.pad(a_up_f, pad_f), a_b_f[None], cos, sin,
                        dk=dk, dv=dv, reverse=False)
    obc, obl = gla_scan(p_c, lr_c, p_l, lr_l, jnp.pad(a_up_b, pad_b), a_b_b[None], cos, sin,
                        dk=dk, dv=dv, reverse=True)
    w_out_b = w_out.astype(BF16)
    g_off, u_off = 2 * dk + dv, 2 * dk + 2 * dv
    x_lat = matmul2_gated_residual(gla_output(ofl, obl, p_l, gla_norm_g, g_off=g_off),
                                   fourier_mix(p_l, u_off=u_off, fd=fd), w_out_b, x_lat, mod_lat[2])
    if not last:
        x_ctx = matmul2_gated_residual(gla_output(ofc, obc, p_c, gla_norm_g, g_off=g_off),
                                       fourier_mix(p_c, u_off=u_off, fd=fd), w_out_b, x_ctx, mod_ctx[2])
    return x_ctx, x_lat


def _ctx_attn_body(q_ref, k_ref, v_ref, o_ref, *, heads, hd):
    nt = (((1,), (1,)), ((), ()))
    for h in range(heads):
        sl = slice(h * hd, (h + 1) * hd)
        s = lax.dot_general(q_ref[0, :, sl], k_ref[0, :, sl], nt, preferred_element_type=F32)
        p = jnp.exp(s - jnp.max(s, axis=-1, keepdims=True))
        o = jnp.dot(p.astype(BF16), v_ref[0, :, sl], preferred_element_type=F32)
        o_ref[0, :, sl] = (o / jnp.sum(p, axis=-1, keepdims=True)).astype(o_ref.dtype)


def context_self_attention(qkv_ctx):
    b, l, d3 = qkv_ctx.shape
    d = d3 // 3
    hd = d // NA_HEADS
    g = NA_HEADS_PER_STEP
    gw = g * hd
    nhg = NA_HEADS // g
    return pl.pallas_call(
        functools.partial(_ctx_attn_body, heads=g, hd=hd),
        grid=(b, nhg),
        in_specs=[pl.BlockSpec((1, l, gw), lambda bi, hg: (bi, 0, hg)),
                  pl.BlockSpec((1, l, gw), lambda bi, hg: (bi, 0, nhg + hg)),
                  pl.BlockSpec((1, l, gw), lambda bi, hg: (bi, 0, 2 * nhg + hg))],
        out_specs=pl.BlockSpec((1, l, gw), lambda bi, hg: (bi, 0, hg)),
        out_shape=jax.ShapeDtypeStruct((b, l, d), BF16),
        compiler_params=_cparams("parallel", "parallel"),
        name="context_self_attention",
    )(qkv_ctx, qkv_ctx, qkv_ctx)


def mixer_neighbourhood(x_ctx, x_lat, mod_ctx, mod_lat, norm_g1, w_qkv, w_out, rpb, last):
    bsz, t, d = x_lat.shape
    l = x_ctx.shape[1]
    hd = d // NA_HEADS
    w_qkv_b = w_qkv.astype(BF16)
    w_out_b = w_out.astype(BF16)
    cs = jnp.concatenate([jnp.full((d,), hd ** -0.5, F32), jnp.ones((2 * d,), F32)])
    qkv_l = norm_mod_matmul(x_lat, norm_g1, mod_lat[0], mod_lat[1], w_qkv_b, cs)
    qkv_c = norm_mod_matmul(x_ctx, norm_g1, mod_ctx[0], mod_ctx[1], w_qkv_b, cs)
    o_l = neighbourhood_attention(qkv_l, qkv_c, rpb)
    x_lat = matmul_gated_residual(o_l, w_out_b, x_lat, mod_lat[2])
    if not last:
        x_ctx = matmul_gated_residual(context_self_attention(qkv_c), w_out_b, x_ctx, mod_ctx[2])
    return x_ctx, x_lat


def kernel(x, c, ctx, c_ctx, ada_w, ada_b, norm1_g, norm2_g, ab_w_in, ab_w_out, gla_a_up_f, gla_a_b_f,
           gla_a_up_b, gla_a_b_b, gla_norm_g, na_w_qkv, na_w_out, na_rpb, router_w, exp_w1, exp_w3,
           exp_w2, final_g):
    depth = ada_w.shape[0]
    bsz, t, d = x.shape
    cond = jnp.concatenate([c, c_ctx[None], jnp.zeros((8 - bsz - 1, d), F32)], axis=0)
    mods = ada_all_layers(cond, ada_w, ada_b)
    w1, w3, w2 = exp_w1.astype(BF16), exp_w3.astype(BF16), exp_w2.astype(BF16)
    xl, xc = x, ctx
    for l in range(depth):
        last = l == depth - 1
        m = mods[l].reshape(8, N_MOD, d)
        mod_lat = [m[:bsz, i][:, None, :] for i in range(N_MOD)]
        mod_ctx = [jnp.broadcast_to(m[bsz, i][None, None, :], (bsz, 1, d)) for i in range(N_MOD)]
        if l % 2 == 0:
            e = l // 2
            xc, xl = mixer_gla_fnet(xc, xl, mod_ctx[:3], mod_lat[:3], norm1_g[l], ab_w_in[e], ab_w_out[e],
                                    gla_a_up_f[e], gla_a_b_f[e], gla_a_up_b[e], gla_a_b_b[e],
                                    gla_norm_g[e], last)
        else:
            o = l // 2
            xc, xl = mixer_neighbourhood(xc, xl, mod_ctx[:3], mod_lat[:3], norm1_g[l], na_w_qkv[o],
                                         na_w_out[o], na_rpb[o], last)
        xl = expert_choice_moe(xl, norm2_g[l], mod_lat[3], mod_lat[4], mod_lat[5], router_w[l], w1, w3, w2, l)
        if not last:
            xc = expert_choice_moe(xc, norm2_g[l], mod_ctx[3], mod_ctx[4], mod_ctx[5], router_w[l], w1, w3, w2, l)
    return rmsnorm_final(xl, final_g)
```

```python
import functools

import jax
import jax.numpy as jnp
import numpy as np
from jax import lax
from jax.experimental import pallas as pl
from jax.experimental.pallas import tpu as pltpu

F32 = jnp.float32
BF16 = jnp.bfloat16

GRID_W = 64
N_MOD = 6
EPS = 1e-6
GLA_HEADS = 4
GLA_RANK = 16
GLA_GATE_NORM = 16.0
GLA_CHUNK = 64
ROPE_BASE = 10000.0
FNET_GROUPS = 4
NA_HEADS = 16
NA_KR_MAX = 8
NA_KC = 16
N_EXPERTS = 16
EC_CAPACITY_FACTOR = 2

LANES = 128
VMEM_LIMIT_BYTES = 56 * 1024 * 1024
MASK_VALUE = -1e30

NA_ROWS_PER_BLOCK = 4
NA_HEADS_PER_STEP = 8


def _cparams(*sem):
    return pltpu.CompilerParams(dimension_semantics=sem, vmem_limit_bytes=VMEM_LIMIT_BYTES)


def _row_tile(t, cap):
    if t <= cap:
        return t
    for step in (LANES, 8):
        for tm in range(cap - cap % step, 0, -step):
            if t % tm == 0:
                return tm
    raise ValueError(f"no tile for {t} under {cap}")


def _ada_body(c_ref, w_ref, b_ref, o_ref):
    c = c_ref[...]
    a = (c * jax.nn.sigmoid(c)).astype(BF16)
    acc = jnp.dot(a, w_ref[0].astype(BF16), preferred_element_type=F32)
    o_ref[0] = acc + b_ref[0]


def ada_all_layers(cond, ada_w, ada_b):
    depth, d, n = ada_w.shape
    r = cond.shape[0]
    tn = _row_tile(n, 1024)
    return pl.pallas_call(
        _ada_body,
        grid=(depth, n // tn),
        in_specs=[
            pl.BlockSpec((r, d), lambda l, j: (0, 0)),
            pl.BlockSpec((1, d, tn), lambda l, j: (l, 0, j)),
            pl.BlockSpec((1, 1, tn), lambda l, j: (l, 0, j)),
        ],
        out_specs=pl.BlockSpec((1, r, tn), lambda l, j: (l, 0, j)),
        out_shape=jax.ShapeDtypeStruct((depth, r, n), F32),
        compiler_params=_cparams("parallel", "parallel"),
        name="ada_params",
    )(cond, ada_w, ada_b.reshape(depth, 1, n))


def _norm_mod(x, g, shift, scale):
    ms = jnp.mean(x * x, axis=-1, keepdims=True)
    y = x * lax.rsqrt(ms + EPS)
    return (y * g) * (1.0 + scale) + shift


def _nm_mm_body(x_ref, g_ref, sh_ref, sc_ref, w_ref, cs_ref, o_ref, h_scr):
    @pl.when(pl.program_id(2) == 0)
    def _():
        h = _norm_mod(x_ref[0], g_ref[...], sh_ref[0], sc_ref[0])
        h_scr[...] = h.astype(h_scr.dtype)

    acc = jnp.dot(h_scr[...], w_ref[...], preferred_element_type=F32)
    o_ref[0] = (acc * cs_ref[...]).astype(o_ref.dtype)


def _nm_mm_aux_body(x_ref, g_ref, sh_ref, sc_ref, w_ref, cs_ref, wa_ref, o_ref, oa_ref, h_scr):
    @pl.when(pl.program_id(2) == 0)
    def _():
        h = _norm_mod(x_ref[0], g_ref[...], sh_ref[0], sc_ref[0])
        h_scr[...] = h.astype(h_scr.dtype)
        oa_ref[0] = jnp.dot(h_scr[...], wa_ref[...], preferred_element_type=F32)

    acc = jnp.dot(h_scr[...], w_ref[...], preferred_element_type=F32)
    o_ref[0] = (acc * cs_ref[...]).astype(o_ref.dtype)


def norm_mod_matmul(x, g, shift, scale, w, col_scale, *, out_dtype=BF16, w_aux=None, tm=1024, tn=2048):
    b, t, d = x.shape
    n = w.shape[1]
    tm = _row_tile(t, tm)
    tn = _row_tile(n, tn)
    in_specs = [
        pl.BlockSpec((1, tm, d), lambda bi, i, j: (bi, i, 0)),
        pl.BlockSpec((1, d), lambda bi, i, j: (0, 0)),
        pl.BlockSpec((1, 1, d), lambda bi, i, j: (bi, 0, 0)),
        pl.BlockSpec((1, 1, d), lambda bi, i, j: (bi, 0, 0)),
        pl.BlockSpec((d, tn), lambda bi, i, j: (0, j)),
        pl.BlockSpec((1, tn), lambda bi, i, j: (0, j)),
    ]
    args = [x, g.reshape(1, d), shift, scale, w, col_scale.reshape(1, n)]
    out_specs = pl.BlockSpec((1, tm, tn), lambda bi, i, j: (bi, i, j))
    out_shape = jax.ShapeDtypeStruct((b, t, n), out_dtype)
    body = _nm_mm_body
    if w_aux is not None:
        na = w_aux.shape[1]
        in_specs.append(pl.BlockSpec((d, na), lambda bi, i, j: (0, 0)))
        args.append(w_aux)
        out_specs = (out_specs, pl.BlockSpec((1, tm, na), lambda bi, i, j: (bi, i, 0)))
        out_shape = (out_shape, jax.ShapeDtypeStruct((b, t, na), F32))
        body = _nm_mm_aux_body
    return pl.pallas_call(
        body,
        grid=(b, t // tm, n // tn),
        in_specs=in_specs,
        out_specs=out_specs,
        out_shape=out_shape,
        scratch_shapes=[pltpu.VMEM((tm, d), BF16)],
        compiler_params=_cparams("parallel", "parallel", "arbitrary"),
        name="norm_mod_matmul",
    )(*args)


def _nm_router_body(x_ref, g_ref, sh_ref, sc_ref, rw_ref, h_ref, lg_ref):
    h = _norm_mod(x_ref[0], g_ref[...], sh_ref[0], sc_ref[0])
    bits = lax.bitcast_convert_type(h.astype(BF16).astype(F32), jnp.uint32)
    half = bits.shape[1] // 2
    h_ref[0] = lax.shift_right_logical(bits[:, :half], jnp.uint32(16)) | bits[:, half:]
    lg = jnp.dot(h, rw_ref[...], preferred_element_type=F32, precision=lax.Precision.HIGHEST)
    lg_ref[0] = lg.T[:lg_ref.shape[1]]


def norm_mod_router(x, g, shift, scale, router_w, *, tm=512):
    b, t, d = x.shape
    e = router_w.shape[1]
    tm = _row_tile(t, tm)
    assert d % (2 * LANES) == 0
    return pl.pallas_call(
        _nm_router_body,
        grid=(b, t // tm),
        in_specs=[
            pl.BlockSpec((1, tm, d), lambda bi, i: (bi, i, 0)),
            pl.BlockSpec((1, d), lambda bi, i: (0, 0)),
            pl.BlockSpec((1, 1, d), lambda bi, i: (bi, 0, 0)),
            pl.BlockSpec((1, 1, d), lambda bi, i: (bi, 0, 0)),
            pl.BlockSpec((d, LANES), lambda bi, i: (0, 0)),
        ],
        out_specs=(
            pl.BlockSpec((1, tm, d // 2), lambda bi, i: (bi, i, 0)),
            pl.BlockSpec((1, e, tm), lambda bi, i: (bi, 0, i)),
        ),
        out_shape=(jax.ShapeDtypeStruct((b, t, d // 2), jnp.uint32), jax.ShapeDtypeStruct((b, e, t), F32)),
        compiler_params=_cparams("parallel", "parallel"),
        name="norm_mod_router",
    )(x, g.reshape(1, d), shift, scale, jnp.pad(router_w, ((0, 0), (0, LANES - e))))


def _route_body(lg_ref, u_ref, lb_ref, ui_ref, pos_ref, gs_ref, idx_ref, rs_ref, *, cap, t_valid):
    _, e, r, _ = lg_ref.shape
    lg = lg_ref[0]
    mx = jnp.max(lg, axis=0, keepdims=True)
    ex = jnp.exp(lg - mx)
    aff = ex / jnp.sum(ex, axis=0, keepdims=True)
    if t_valid < r * LANES:
        tok = (lax.broadcasted_iota(jnp.int32, (e, r, LANES), 1) * LANES
               + lax.broadcasted_iota(jnp.int32, (e, r, LANES), 2))
        aff = jnp.where(tok < t_valid, aff, -1.0)
    bits = lax.bitcast_convert_type(aff, jnp.int32)

    def count(mask):
        s1 = jnp.sum(jnp.where(mask, 1.0, 0.0), axis=1, keepdims=True)
        return jnp.sum(s1, axis=2, keepdims=True)

    def search(i, prefix):
        cand = prefix | lax.shift_left(jnp.int32(1), 30 - i)
        return jnp.where(count(bits >= cand) >= cap, cand, prefix)

    thr = lax.fori_loop(0, 31, search, jnp.zeros((e, 1, 1), jnp.int32))

    def cumsum_incl(mask):
        x2 = jnp.where(mask, 1.0, 0.0).reshape(e * r, LANES).astype(BF16)
        within = jnp.dot(x2, u_ref[...], preferred_element_type=F32)
        rowtot = jnp.broadcast_to(within[:, LANES - 1:LANES], (e * r, LANES)).astype(BF16)
        rowoff = jnp.dot(lb_ref[...], rowtot, preferred_element_type=F32)
        return within.reshape(e, r, LANES), rowoff.reshape(e, r, LANES)

    gt = bits > thr
    eq = bits == thr
    need = cap - count(gt)
    w_eq, o_eq = cumsum_incl(eq)
    sel = gt | (eq & (w_eq + o_eq <= need))
    within, rowoff = cumsum_incl(sel)
    pos = within + rowoff - 1.0
    pos_ref[0] = jnp.where(sel, pos, -1.0).astype(jnp.int32)
    gs_ref[0] = jnp.where(sel, aff, 0.0)
    rs_ref[0] = rowoff.astype(jnp.int32)

    s_col = lax.broadcasted_iota(jnp.int32, (cap, 1), 0).astype(F32)
    ones8 = jnp.ones((8, LANES), BF16)
    lane_r = lax.broadcasted_iota(jnp.int32, (cap, r), 1).astype(F32)
    for ei in range(e):
        sel_e = jnp.where(sel[ei], 1.0, 0.0).astype(BF16)
        rowtot_l = lax.dot_general(ones8, sel_e, (((1,), (1,)), ((), ())),
                                   preferred_element_type=F32)
        rowend_l = jnp.dot(rowtot_l.astype(BF16), ui_ref[...], preferred_element_type=F32)
        before = rowend_l[0:1, :] <= s_col
        row_s = jnp.sum(jnp.where(before, 1.0, 0.0), axis=1, keepdims=True)
        off_s = jnp.sum(jnp.where(before, rowtot_l[0:1, :], 0.0), axis=1, keepdims=True)
        onehot = jnp.where(lane_r == row_s, 1.0, 0.0).astype(BF16)
        w_rows = jnp.dot(onehot, within[ei].astype(BF16), preferred_element_type=F32)
        j_s = jnp.sum(jnp.where(w_rows <= s_col - off_s, 1.0, 0.0), axis=1, keepdims=True)
        idx_ref[0, ei] = jnp.broadcast_to(row_s * LANES + j_s, (cap, LANES)).astype(jnp.int32)


def route(logits, cap):
    b, e, t = logits.shape
    r = max(-(-t // LANES), 8)
    tp = r * LANES
    lg = jnp.pad(logits, ((0, 0), (0, 0), (0, tp - t))).reshape(b, e, r, LANES)
    tri = np.arange(LANES)[:, None] <= np.arange(LANES)[None, :]
    u = jnp.asarray(tri, BF16)
    ii = np.arange(e * r)
    lb = jnp.asarray((ii[:, None] // r == ii[None, :] // r) & (ii[None, :] % r < ii[:, None] % r), BF16)
    ui = jnp.asarray(np.arange(r)[:, None] <= np.arange(r)[None, :], BF16)
    blk = pl.BlockSpec((1, e, r, LANES), lambda bi: (bi, 0, 0, 0))
    pos, gs, idx, rs = pl.pallas_call(
        functools.partial(_route_body, cap=cap, t_valid=t),
        grid=(b,),
        in_specs=[blk,
                  pl.BlockSpec((LANES, LANES), lambda bi: (0, 0)),
                  pl.BlockSpec((e * r, e * r), lambda bi: (0, 0)),
                  pl.BlockSpec((r, r), lambda bi: (0, 0))],
        out_specs=(blk, blk, pl.BlockSpec((1, e, cap, LANES), lambda bi: (bi, 0, 0, 0)), blk),
        out_shape=(jax.ShapeDtypeStruct((b, e, r, LANES), jnp.int32),
                   jax.ShapeDtypeStruct((b, e, r, LANES), F32),
                   jax.ShapeDtypeStruct((b, e, cap, LANES), jnp.int32),
                   jax.ShapeDtypeStruct((b, e, r, LANES), jnp.int32)),
        compiler_params=_cparams("parallel"),
        name="route",
    )(lg, u, lb, ui)
    return (pos.reshape(b, e, tp)[:, :, :t], gs.reshape(b, e, tp)[:, :, :t], idx[..., 0], rs[..., 0])


def _mm_res_body(a_ref, w_ref, r_ref, g_ref, o_ref):
    acc = jnp.dot(a_ref[0], w_ref[...], preferred_element_type=F32)
    o_ref[0] = r_ref[0] + g_ref[0] * acc


def matmul_gated_residual(a, w, res, gate, *, tm=512, tn=2048):
    b, t, k = a.shape
    n = w.shape[1]
    tm = _row_tile(t, tm)
    tn = _row_tile(n, tn)
    return pl.pallas_call(
        _mm_res_body,
        grid=(b, t // tm, n // tn),
        in_specs=[
            pl.BlockSpec((1, tm, k), lambda bi, i, j: (bi, i, 0)),
            pl.BlockSpec((k, tn), lambda bi, i, j: (0, j)),
            pl.BlockSpec((1, tm, tn), lambda bi, i, j: (bi, i, j)),
            pl.BlockSpec((1, 1, tn), lambda bi, i, j: (bi, 0, j)),
        ],
        out_specs=pl.BlockSpec((1, tm, tn), lambda bi, i, j: (bi, i, j)),
        out_shape=jax.ShapeDtypeStruct((b, t, n), F32),
        compiler_params=_cparams("parallel", "parallel", "parallel"),
        name="matmul_gated_residual",
    )(a, w, res, gate)


def _rmsnorm_body(x_ref, g_ref, o_ref):
    x = x_ref[0]
    ms = jnp.mean(x * x, axis=-1, keepdims=True)
    o_ref[0] = (x * lax.rsqrt(ms + EPS)) * g_ref[...]


def rmsnorm_final(x, g, *, tm=512):
    b, t, d = x.shape
    tm = _row_tile(t, tm)
    return pl.pallas_call(
        _rmsnorm_body,
        grid=(b, t // tm),
        in_specs=[pl.BlockSpec((1, tm, d), lambda bi, i: (bi, i, 0)),
                  pl.BlockSpec((1, d), lambda bi, i: (0, 0))],
        out_specs=pl.BlockSpec((1, tm, d), lambda bi, i: (bi, i, 0)),
        out_shape=jax.ShapeDtypeStruct((b, t, d), F32),
        compiler_params=_cparams("parallel", "parallel"),
        name="rmsnorm_final",
    )(x, g.reshape(1, d))


def _na_body(q_ref, k0_ref, k1_ref, k2_ref, v0_ref, v1_ref, v2_ref, kc_ref, vc_ref, bias_ref, o_ref,
             *, heads, hd, qb):
    dn = (((1,), (1,)), ((), ()))
    k_refs = (k0_ref, k1_ref, k2_ref)
    v_refs = (v0_ref, v1_ref, v2_ref)
    for g in range(heads):
        sl = slice(g * hd, (g + 1) * hd)
        q = q_ref[0, :, sl]
        s = [lax.dot_general(q, k_refs[i][0, :, sl], dn, preferred_element_type=F32)
             + bias_ref[0, g, :, i * qb:(i + 1) * qb] for i in range(3)]
        s.append(lax.dot_general(q, kc_ref[0, :, sl], dn, preferred_element_type=F32))
        m = functools.reduce(jnp.maximum, [jnp.max(si, axis=-1, keepdims=True) for si in s])
        p = [jnp.exp(si - m) for si in s]
        l = functools.reduce(jnp.add, [jnp.sum(pi, axis=-1, keepdims=True) for pi in p])
        vals = [v_refs[i][0, :, sl] for i in range(3)] + [vc_ref[0, :, sl]]
        o = functools.reduce(jnp.add, [jnp.dot(pi.astype(BF16), vi, preferred_element_type=F32)
                                       for pi, vi in zip(p, vals)])
        o_ref[0, :, sl] = (o / l).astype(o_ref.dtype)


def _na_bias_table(rpb, rows):
    r_blk = NA_ROWS_PER_BLOCK
    nblk = rows // r_blk
    kr = min(NA_KR_MAX, rows)
    tables = []
    for rb in (0, 1, nblk - 1):
        ks = min(max(rb - 1, 0), nblk - 3) * r_blk
        q_row = rb * r_blk + np.arange(r_blk)
        rs = np.clip(q_row - kr // 2, 0, rows - kr)
        k_row = ks + np.arange(3 * r_blk)
        ok_row = (k_row[None, :] >= rs[:, None]) & (k_row[None, :] < rs[:, None] + kr)
        rel_row = np.clip(k_row[None, :] - q_row[:, None] + (NA_KR_MAX - 1), 0, 2 * NA_KR_MAX - 2)
        q_col = np.arange(GRID_W)
        cs = np.clip(q_col - NA_KC // 2, 0, GRID_W - NA_KC)
        k_col = np.arange(GRID_W)
        ok_col = (k_col[None, :] >= cs[:, None]) & (k_col[None, :] < cs[:, None] + NA_KC)
        rel_col = np.clip(k_col[None, :] - q_col[:, None] + (NA_KC - 1), 0, 2 * NA_KC - 2)
        hi = lax.Precision.HIGHEST
        oh_r = jnp.asarray(rel_row[:, :, None] == np.arange(2 * NA_KR_MAX - 1), F32)
        oh_c = jnp.asarray(np.arange(2 * NA_KC - 1)[:, None, None] == rel_col[None], F32)
        t = jnp.einsum('qkr,hrc->hqkc', oh_r, rpb, precision=hi)
        t = jnp.einsum('hqkc,cwv->hqwkv', t, oh_c, precision=hi)
        ok = ok_row[:, None, :, None] & ok_col[None, :, None, :]
        t = jnp.where(ok[None], t, MASK_VALUE)
        tables.append(t.reshape(rpb.shape[0], r_blk * GRID_W, 3 * r_blk * GRID_W))
    return jnp.stack(tables)


def neighbourhood_attention(qkv, qkv_ctx, rpb):
    b, t, d3 = qkv.shape
    d = d3 // 3
    l = qkv_ctx.shape[1]
    hd = d // NA_HEADS
    rows = t // GRID_W
    r_blk = NA_ROWS_PER_BLOCK
    qb = r_blk * GRID_W
    nblk = rows // r_blk
    assert rows % r_blk == 0 and nblk >= 3 and min(NA_KR_MAX, rows) + r_blk <= 3 * r_blk
    g = NA_HEADS_PER_STEP
    gw = g * hd
    nhg = NA_HEADS // g
    bias = _na_bias_table(rpb.astype(F32), rows)

    def kstart(rb):
        return jnp.clip(rb - 1, 0, nblk - 3)

    def pattern(rb):
        return jnp.where(rb == 0, 0, jnp.where(rb == nblk - 1, 2, 1))

    q_spec = pl.BlockSpec((1, qb, gw), lambda hg, bi, rb: (bi, rb, hg))
    k_specs = [pl.BlockSpec((1, qb, gw), functools.partial(
        lambda hg, bi, rb, i: (bi, kstart(rb) + i, nhg + hg), i=i)) for i in range(3)]
    v_specs = [pl.BlockSpec((1, qb, gw), functools.partial(
        lambda hg, bi, rb, i: (bi, kstart(rb) + i, 2 * nhg + hg), i=i)) for i in range(3)]
    kc_spec = pl.BlockSpec((1, l, gw), lambda hg, bi, rb: (bi, 0, nhg + hg))
    vc_spec = pl.BlockSpec((1, l, gw), lambda hg, bi, rb: (bi, 0, 2 * nhg + hg))
    bias_spec = pl.BlockSpec((1, g, qb, 3 * qb), lambda hg, bi, rb: (pattern(rb), hg, 0, 0))
    return pl.pallas_call(
        functools.partial(_na_body, heads=g, hd=hd, qb=qb),
        grid=(nhg, b, nblk),
        in_specs=[q_spec] + k_specs + v_specs + [kc_spec, vc_spec, bias_spec],
        out_specs=pl.BlockSpec((1, qb, gw), lambda hg, bi, rb: (bi, rb, hg)),
        out_shape=jax.ShapeDtypeStruct((b, t, d), BF16),
        compiler_params=_cparams("parallel", "parallel", "arbitrary"),
        name="neighbourhood_attention",
    )(qkv, qkv, qkv, qkv, qkv, qkv, qkv, qkv_ctx, qkv_ctx, bias)


MOE_GATHER_UNROLL = 8
MOE_ISSUE_GROUPS = 8


def _moe_ffn_body(first_ref, next_ref, h_hbm, w1_ref, w3_ref, w2_ref, o_ref, xbuf, xb_scr, sem):
    i = pl.program_id(1)
    n_tiles = pl.num_programs(1)
    step = pl.program_id(0) * n_tiles + i
    tm = xbuf.shape[1]
    slot = lax.rem(step, 2)

    def row_copy(src_row, sl, r):
        return pltpu.make_async_copy(h_hbm.at[pl.ds(src_row, 1)], xbuf.at[sl, pl.ds(r, 1)], sem.at[sl])

    @pl.when(step == 0)
    def _():
        def body(r, carry):
            row_copy(first_ref[0, 0, r], slot, r).start()
            return carry
        lax.fori_loop(0, tm, body, 0, unroll=MOE_GATHER_UNROLL)

    pltpu.make_async_copy(h_hbm.at[pl.ds(0, tm)], xbuf.at[slot], sem.at[slot]).wait()
    packed = xbuf[slot]
    half = packed.shape[1]
    xb_scr[:, :half] = lax.bitcast_convert_type(lax.shift_left(packed, jnp.uint32(16)), F32).astype(BF16)
    xb_scr[:, half:] = lax.bitcast_convert_type(packed & jnp.uint32(0xFFFF0000), F32).astype(BF16)

    per_group = tm // MOE_ISSUE_GROUPS

    def issue_group(gi):
        for r in range(gi * per_group, (gi + 1) * per_group):
            row_copy(next_ref[0, 0, i * tm + r], 1 - slot, r).start()

    n_chunks = MOE_ISSUE_GROUPS // 2
    fc = w1_ref.shape[3] // n_chunks
    parts = []
    for c in range(n_chunks):
        cols = slice(c * fc, (c + 1) * fc)
        issue_group(2 * c)
        a = jnp.dot(xb_scr[...], w1_ref[0, 0, :, cols], preferred_element_type=F32)
        issue_group(2 * c + 1)
        bb = jnp.dot(xb_scr[...], w3_ref[0, 0, :, cols], preferred_element_type=F32)
        parts.append((a * jax.nn.sigmoid(a) * bb).astype(BF16))
    hm = jnp.concatenate(parts, axis=1)
    o_ref[0] = jnp.dot(hm, w2_ref[0, 0], preferred_element_type=F32).astype(o_ref.dtype)

    @pl.when(step == pl.num_programs(0) * n_tiles - 1)
    def _():
        pltpu.make_async_copy(h_hbm.at[pl.ds(0, tm)], xbuf.at[1 - slot], sem.at[1 - slot]).wait()


def moe_ffn(h, rows, w1, w3, w2, layer, *, tm=512):
    e, m = rows.shape
    d = 2 * h.shape[1]
    f = w1.shape[3]
    tm = _row_tile(m, tm)
    assert tm % MOE_ISSUE_GROUPS == 0 and f % (MOE_ISSUE_GROUPS // 2 * LANES) == 0
    rows_next = jnp.roll(rows.reshape(-1), -tm).reshape(e, 1, m)
    rows_first = rows[0, :tm].reshape(1, 1, tm)
    return pl.pallas_call(
        _moe_ffn_body,
        grid=(e, m // tm),
        in_specs=[
            pl.BlockSpec((1, 1, tm), lambda ei, i: (0, 0, 0), memory_space=pltpu.SMEM),
            pl.BlockSpec((1, 1, m), lambda ei, i: (ei, 0, 0), memory_space=pltpu.SMEM),
            pl.BlockSpec(memory_space=pl.ANY),
            pl.BlockSpec((1, 1, d, f), lambda ei, i: (layer, ei, 0, 0)),
            pl.BlockSpec((1, 1, d, f), lambda ei, i: (layer, ei, 0, 0)),
            pl.BlockSpec((1, 1, f, d), lambda ei, i: (layer, ei, 0, 0)),
        ],
        out_specs=pl.BlockSpec((1, tm, d), lambda ei, i: (ei, i, 0)),
        out_shape=jax.ShapeDtypeStruct((e, m, d), BF16),
        scratch_shapes=[pltpu.VMEM((2, tm, d // 2), jnp.uint32), pltpu.VMEM((tm, d), BF16),
                        pltpu.SemaphoreType.DMA((2,))],
        compiler_params=_cparams("arbitrary", "arbitrary"),
        name="moe_ffn",
    )(rows_first, rows_next, h, w1, w3, w2)


COMBINE_TOKENS = 256
COMBINE_WINDOW = 64
BF16_SUBLANES = 16


def _combine_body(ts_ref, pos_ref, gs_ref, x_ref, g2_ref, ye_hbm, o_ref, ybuf, yextra, sem, sem_x,
                  *, n_exp, cap, win, n_batch):
    bi = pl.program_id(0)
    ti = pl.program_id(1)
    nt = pl.num_programs(1)
    step = bi * nt + ti
    total = n_batch * nt
    slot = lax.rem(step, 2)
    tt = pos_ref.shape[2]

    def tile_start(b, t, e):
        return ts_ref[(b * (nt + 1) + t) * n_exp + e]

    def window_start(b, t, e, w):
        st = tile_start(b, t, e)
        a = lax.shift_left(lax.shift_right_logical(st, 4), 4) + w * win
        return jnp.minimum(a, cap - win)

    def window_copy(b, e, a, dst, dsem):
        row0 = pl.multiple_of((e * n_batch + b) * cap + a, BF16_SUBLANES)
        return pltpu.make_async_copy(ye_hbm.at[pl.ds(row0, win)], dst, dsem)

    def issue(b, t, sl):
        for e in range(n_exp):
            window_copy(b, e, window_start(b, t, e, 0), ybuf.at[sl, e], sem.at[sl]).start()

    @pl.when(step == 0)
    def _():
        issue(bi, ti, slot)

    for e in range(n_exp):
        window_copy(bi, e, 0, ybuf.at[slot, e], sem.at[slot]).wait()

    @pl.when(step + 1 < total)
    def _():
        nxt = step + 1
        issue(nxt // nt, lax.rem(nxt, nt), 1 - slot)

    m_iota = lax.broadcasted_iota(jnp.int32, (win, tt), 0)
    dn = (((0,), (0,)), ((), ()))

    def gather_matrix(w):
        rows = []
        for e in range(n_exp):
            a = window_start(bi, ti, e, w)
            lo = window_start(bi, ti, e, 0) + w * win
            p = pos_ref[0, e:e + 1, :]
            hit = (p - a == m_iota) & (p >= lo)
            rows.append(jnp.where(hit, gs_ref[0, e:e + 1, :], 0.0).astype(BF16))
        return jnp.concatenate(rows, axis=0)

    d = ybuf.shape[-1]
    acc = lax.dot_general(gather_matrix(0), ybuf[slot].reshape(n_exp * win, d), dn,
                          preferred_element_type=F32)

    n_win = jnp.int32(1)
    for e in range(n_exp):
        span = tile_start(bi, ti + 1, e) - window_start(bi, ti, e, 0)
        n_win = jnp.maximum(n_win, (span + win - 1) // win)

    def extra(w, acc):
        for e in range(n_exp):
            cp = window_copy(bi, e, window_start(bi, ti, e, w), yextra.at[e], sem_x)
            cp.start()
            cp.wait()
        return acc + lax.dot_general(gather_matrix(w), yextra[...].reshape(n_exp * win, d), dn,
                                     preferred_element_type=F32)

    acc = lax.fori_loop(1, n_win, extra, acc)
    o_ref[0] = x_ref[0] + g2_ref[0] * acc


def moe_combine(x, g2, ye, pos, gs, tile_start, cap):
    b, t, d = x.shape
    e = pos.shape[1]
    tt = min(COMBINE_TOKENS, t)
    win = min(COMBINE_WINDOW, cap)
    nt = t // tt
    grid_spec = pltpu.PrefetchScalarGridSpec(
        num_scalar_prefetch=1,
        grid=(b, nt),
        in_specs=[
            pl.BlockSpec((1, e, tt), lambda bi, ti, ts: (bi, 0, ti)),
            pl.BlockSpec((1, e, tt), lambda bi, ti, ts: (bi, 0, ti)),
            pl.BlockSpec((1, tt, d), lambda bi, ti, ts: (bi, ti, 0)),
            pl.BlockSpec((1, 1, d), lambda bi, ti, ts: (bi, 0, 0)),
            pl.BlockSpec(memory_space=pl.ANY),
        ],
        out_specs=pl.BlockSpec((1, tt, d), lambda bi, ti, ts: (bi, ti, 0)),
        scratch_shapes=[
            pltpu.VMEM((2, e, win, d), BF16),
            pltpu.VMEM((e, win, d), BF16),
            pltpu.SemaphoreType.DMA((2,)),
            pltpu.SemaphoreType.DMA(()),
        ],
    )
    return pl.pallas_call(
        functools.partial(_combine_body, n_exp=e, cap=cap, win=win, n_batch=b),
        grid_spec=grid_spec,
        out_shape=jax.ShapeDtypeStruct((b, t, d), F32),
        compiler_params=_cparams("arbitrary", "arbitrary"),
        name="moe_combine",
    )(tile_start, pos, gs, x, g2, ye)


def expert_choice_moe(x, g, shift, scale, g2, router_w, w1, w3, w2, layer):
    b, t, d = x.shape
    e = router_w.shape[1]
    cap = EC_CAPACITY_FACTOR * t // e
    h, logits = norm_mod_router(x, g, shift, scale, router_w)
    pos, gs, idx, rowstart = route(logits, cap)
    rows = idx + (jnp.arange(b, dtype=jnp.int32) * t)[:, None, None]
    rows = jnp.swapaxes(rows, 0, 1).reshape(e, b * cap)
    ye = moe_ffn(h.reshape(b * t, d // 2), rows, w1, w3, w2, layer).reshape(e * b * cap, d)
    tt = min(COMBINE_TOKENS, t)
    ts = rowstart[:, :, ::tt // LANES][:, :, :t // tt]
    ts = jnp.concatenate([jnp.swapaxes(ts, 1, 2), jnp.full((b, 1, e), cap, jnp.int32)], axis=1)
    return moe_combine(x, g2, ye, pos, gs, ts.reshape(-1), cap)


def _axial_rope_angles(n_tok, head_dim):
    n_freq = head_dim // 4
    inv = ROPE_BASE ** (-jnp.arange(n_freq, dtype=F32) / n_freq)
    tt = jnp.arange(n_tok)
    row = (tt // GRID_W).astype(F32)
    col = (tt % GRID_W).astype(F32)
    return row[:, None] * inv[None], col[:, None] * inv[None]


GLA_BLOCK = 4 * GLA_CHUNK
GLA_HEADS_PER_STEP = 4


def _log_sigmoid(x):
    return jnp.minimum(x, 0.0) - jnp.log(1.0 + jnp.exp(-jnp.abs(x)))


def _gla_block(q_ref, k_ref, v_ref, lr_ref, cos_ref, sin_ref, a_ref, ab_ref, tri_ref, perm_ref, st_ref, o_ref,
               *, reverse, g, hk, hv):
    ks = slice(g * hk, (g + 1) * hk)
    vs = slice(g * hv, (g + 1) * hv)
    nt = (((1,), (1,)), ((), ()))
    tn = (((0,), (0,)), ((), ()))
    hi = lax.Precision.HIGHEST
    n = GLA_BLOCK
    c = GLA_CHUNK
    pre = jnp.dot(lr_ref[0], a_ref[0, :, ks], preferred_element_type=F32, precision=hi) + ab_ref[0, :, ks]
    la = _log_sigmoid(pre) / GLA_GATE_NORM
    la_hi = la.astype(BF16)
    la_r = la - la_hi.astype(F32)
    la_mid = la_r.astype(BF16)
    la_lo = (la_r - la_mid.astype(F32)).astype(BF16)
    tri = tri_ref[...]
    bcum = (jnp.dot(tri, la_hi, preferred_element_type=F32) + jnp.dot(tri, la_mid, preferred_element_type=F32)
            + jnp.dot(tri, la_lo, preferred_element_type=F32))
    q = q_ref[0, :, ks]
    k = k_ref[0, :, ks]
    v = v_ref[0, :, vs]
    qf = q.astype(F32)
    kf = k.astype(F32)
    if cos_ref is not None:
        cs, sn = cos_ref[...], sin_ref[...]
        qf = qf * cs + jnp.dot(q, perm_ref[...], preferred_element_type=F32) * sn
        kf = kf * cs + jnp.dot(k, perm_ref[...], preferred_element_type=F32) * sn
    last_row = [bcum[(i * c if reverse else i * c + c - 1):(i * c + 1 if reverse else i * c + c)]
                for i in range(n // c)]
    blast = jnp.concatenate([jnp.broadcast_to(r, (c, r.shape[1])) for r in last_row], axis=0)
    qe = (qf * jnp.exp(bcum)).astype(BF16)
    ke = (kf * jnp.exp(-bcum)).astype(BF16)
    kd = (kf * jnp.exp(blast - bcum)).astype(BF16)
    row = lax.broadcasted_iota(jnp.int32, (n, n), 0)
    col = lax.broadcasted_iota(jnp.int32, (n, n), 1)
    same_chunk = (row // c) == (col // c)
    causal = (col >= row) if reverse else (col <= row)
    att = jnp.where(same_chunk & causal, lax.dot_general(qe, ke, nt, preferred_element_type=F32), 0.0)
    o_intra = jnp.dot(att.astype(BF16), v, preferred_element_type=F32)
    order = range(n // c - 1, -1, -1) if reverse else range(n // c)
    for i in order:
        sl = slice(i * c, (i + 1) * c)
        st = st_ref[g]
        o_ref[0, 0, sl, vs] = o_intra[sl] + lax.dot_general(qe[sl], st.astype(BF16), nt,
                                                            preferred_element_type=F32)
        u_t = lax.dot_general(v[sl], kd[sl], tn, preferred_element_type=F32)
        st_ref[g] = st * jnp.exp(last_row[i]) + u_t


def _gla_body(qc_ref, kc_ref, vc_ref, lrc_ref, ql_ref, kl_ref, vl_ref, lrl_ref, cos_ref, sin_ref,
              a_ref, ab_ref, tri_ref, perm_ref, oc_ref, ol_ref, st_ref, *, reverse, heads, hk, hv):
    j = pl.program_id(2)

    @pl.when(j == 0)
    def _():
        st_ref[...] = jnp.zeros_like(st_ref)
        for g in range(heads):
            _gla_block(qc_ref, kc_ref, vc_ref, lrc_ref, None, None, a_ref, ab_ref, tri_ref, perm_ref, st_ref,
                       oc_ref, reverse=reverse, g=g, hk=hk, hv=hv)

    @pl.when(j > 0)
    def _():
        for g in range(heads):
            _gla_block(ql_ref, kl_ref, vl_ref, lrl_ref, cos_ref, sin_ref, a_ref, ab_ref, tri_ref, perm_ref,
                       st_ref, ol_ref, reverse=reverse, g=g, hk=hk, hv=hv)


def gla_scan(p_c, lr_c, p_l, lr_l, a_up, a_b, cos, sin, *, dk, dv, reverse):
    b, tc, _ = p_c.shape
    tl = p_l.shape[1]
    hk, hv = dk // GLA_HEADS, dv // GLA_HEADS
    n = GLA_BLOCK
    assert tc == n and tl % n == 0 and hk == LANES
    nl = tl // n
    hps = GLA_HEADS_PER_STEP
    kw, vw = hps * hk, hps * hv
    qo, ko, vo = 0, dk // kw, 2 * dk // vw
    ii = np.arange(n)
    same = ii[:, None] // GLA_CHUNK == ii[None, :] // GLA_CHUNK
    tri = jnp.asarray(same & ((ii[None, :] >= ii[:, None]) if reverse else (ii[None, :] <= ii[:, None])), BF16)
    jj = np.arange(hk)
    partner = np.where((jj % (hk // 2)) < hk // 4, jj + hk // 4, jj - hk // 4)
    perm = np.zeros((hk, hk), np.float32)
    perm[partner, jj] = 1.0
    perm = jnp.asarray(perm, BF16)

    def lat_blk(j):
        blk = jnp.clip(j - 1, 0, nl - 1)
        return (nl - 1 - blk) if reverse else blk

    ctx_spec = lambda width, off: pl.BlockSpec((1, n, width), lambda bi, h, j: (bi, 0, off + h))
    lat_spec = lambda width, off: pl.BlockSpec((1, n, width), lambda bi, h, j: (bi, lat_blk(j), off + h))
    in_specs = [
        ctx_spec(kw, qo), ctx_spec(kw, ko), ctx_spec(vw, vo),
        pl.BlockSpec((1, n, LANES), lambda bi, h, j: (bi, 0, 0)),
        lat_spec(kw, qo), lat_spec(kw, ko), lat_spec(vw, vo),
        pl.BlockSpec((1, n, LANES), lambda bi, h, j: (bi, lat_blk(j), 0)),
        pl.BlockSpec((n, hk), lambda bi, h, j: (lat_blk(j), 0)),
        pl.BlockSpec((n, hk), lambda bi, h, j: (lat_blk(j), 0)),
        pl.BlockSpec((1, LANES, kw), lambda bi, h, j: (0, 0, h)),
        pl.BlockSpec((1, 1, kw), lambda bi, h, j: (0, 0, h)),
        pl.BlockSpec((n, n), lambda bi, h, j: (0, 0)),
        pl.BlockSpec((hk, hk), lambda bi, h, j: (0, 0)),
    ]
    out_specs = (
        pl.BlockSpec((1, 1, n, vw), lambda bi, h, j: (0, bi, 0, h)),
        pl.BlockSpec((1, 1, n, vw), lambda bi, h, j: (0, bi, lat_blk(j), h)),
    )
    o_c, o_l = pl.pallas_call(
        functools.partial(_gla_body, reverse=reverse, heads=hps, hk=hk, hv=hv),
        grid=(b, GLA_HEADS // hps, nl + 1),
        in_specs=in_specs,
        out_specs=out_specs,
        out_shape=(jax.ShapeDtypeStruct((1, b, tc, dv), F32), jax.ShapeDtypeStruct((1, b, tl, dv), F32)),
        scratch_shapes=[pltpu.VMEM((hps, hv, hk), F32)],
        compiler_params=_cparams("parallel", "parallel", "arbitrary"),
        name="gla_scan_bwd" if reverse else "gla_scan_fwd",
    )(p_c, p_c, p_c, lr_c, p_l, p_l, p_l, lr_l, cos, sin, a_up[None], a_b[None], tri, perm)
    return o_c[0], o_l[0]


def _gla_out_body(of_ref, ob_ref, g_ref, ng_ref, y_ref, *, heads):
    o = of_ref[0] + ob_ref[0]
    hv = o.shape[1] // heads
    outs = []
    for h in range(heads):
        oh = o[:, h * hv:(h + 1) * hv]
        outs.append(oh * lax.rsqrt(jnp.mean(oh * oh, axis=-1, keepdims=True) + EPS))
    on = jnp.concatenate(outs, axis=1) * ng_ref[...]
    gg = g_ref[0].astype(F32)
    y_ref[0] = (on * (gg * jax.nn.sigmoid(gg))).astype(y_ref.dtype)


def gla_output(o_f, o_b, p, norm_g, *, g_off, tm=256):
    b, t, dv = o_f.shape
    tm = _row_tile(t, tm)
    blk = pl.BlockSpec((1, tm, dv), lambda bi, i: (bi, i, 0))
    return pl.pallas_call(
        functools.partial(_gla_out_body, heads=GLA_HEADS),
        grid=(b, t // tm),
        in_specs=[blk, blk, pl.BlockSpec((1, tm, dv), lambda bi, i: (bi, i, g_off // dv)),
                  pl.BlockSpec((1, dv), lambda bi, i: (0, 0))],
        out_specs=blk,
        out_shape=jax.ShapeDtypeStruct((b, t, dv), BF16),
        compiler_params=_cparams("parallel", "parallel"),
        name="gla_output",
    )(o_f, o_b, p, norm_g.reshape(1, dv))


FFT_T2 = 128
FFT_ROWS = 8


def _dft_mats(n):
    ang = 2.0 * np.pi * ((np.arange(n)[:, None] * np.arange(n)[None, :]) % n) / n
    return np.cos(ang), np.sin(ang)


def _chan_dft_body(u_ref, w_ref, z_ref):
    z_ref[0] = jnp.dot(u_ref[0], w_ref[...], preferred_element_type=F32)


def channel_dft(p, *, u_off, fd, tm=512):
    b, t, _ = p.shape
    gd = fd // FNET_GROUPS
    c, s = _dft_mats(gd)
    w = jnp.asarray(np.concatenate([c, -s], axis=1), BF16)
    tm = _row_tile(t, tm)
    return pl.pallas_call(
        _chan_dft_body,
        grid=(b, t // tm, FNET_GROUPS),
        in_specs=[pl.BlockSpec((1, tm, gd), lambda bi, i, g: (bi, i, u_off // gd + g)),
                  pl.BlockSpec((gd, 2 * gd), lambda bi, i, g: (0, 0))],
        out_specs=pl.BlockSpec((1, tm, 2 * gd), lambda bi, i, g: (bi, i, g)),
        out_shape=jax.ShapeDtypeStruct((b, t, 2 * fd), F32),
        compiler_params=_cparams("parallel", "parallel", "parallel"),
        name="channel_dft",
    )(p, w)


def _dft_direct_body(z_ref, f_ref, y_ref, *, gd, scale):
    z = z_ref[0]
    zz = jnp.concatenate([z[:, :gd], z[:, gd:]], axis=0).astype(BF16)
    y_ref[0] = jnp.dot(f_ref[...], zz, preferred_element_type=F32) * scale


def token_dft_direct(z, gd):
    b, t, w2 = z.shape
    c, s = _dft_mats(t)
    f = jnp.asarray(np.concatenate([c, s], axis=1), BF16)
    return pl.pallas_call(
        functools.partial(_dft_direct_body, gd=gd, scale=float((t * gd) ** -0.5)),
        grid=(b, FNET_GROUPS),
        in_specs=[pl.BlockSpec((1, t, 2 * gd), lambda bi, g: (bi, 0, g)),
                  pl.BlockSpec((t, 2 * t), lambda bi, g: (0, 0))],
        out_specs=pl.BlockSpec((1, t, gd), lambda bi, g: (bi, 0, g)),
        out_shape=jax.ShapeDtypeStruct((b, t, w2 // 2), F32),
        compiler_params=_cparams("parallel", "parallel"),
        name="token_dft_direct",
    )(z, f)


def _dft_stage1_body(z_ref, f_ref, twc_ref, tws_ref, o_ref, *, gd):
    for j in range(FFT_ROWS):
        z = z_ref[0, :, j, :]
        zz = jnp.concatenate([z[:, :gd], z[:, gd:]], axis=0).astype(BF16)
        a = jnp.dot(f_ref[...], zz, preferred_element_type=F32)
        t1 = a.shape[0] // 2
        ar, ai = a[:t1], a[t1:]
        tc = jnp.concatenate([twc_ref[j]] * (gd // LANES), axis=1)
        ts = jnp.concatenate([tws_ref[j]] * (gd // LANES), axis=1)
        o_ref[0, 0, :, j, :gd] = ar * tc + ai * ts
        o_ref[0, 0, :, j, gd:] = ai * tc - ar * ts


def _dft_stage2_body(b_ref, f_ref, y_ref, *, gd, scale):
    for j in range(FFT_ROWS):
        s = b_ref[0, 0, j]
        ss = jnp.concatenate([s[:, :gd], s[:, gd:]], axis=0).astype(BF16)
        y_ref[0, :, j, :] = jnp.dot(f_ref[...], ss, preferred_element_type=F32) * scale


def token_dft_two_stage(z, gd):
    b, t, w2 = z.shape
    t2 = FFT_T2
    t1 = t // t2
    assert t1 * t2 == t and t1 % FFT_ROWS == 0 and t2 % FFT_ROWS == 0 and gd % LANES == 0
    c1, s1 = _dft_mats(t1)
    f1 = jnp.asarray(np.block([[c1, s1], [-s1, c1]]), BF16)
    ang = 2.0 * np.pi * (np.arange(t2)[:, None] * np.arange(t1)[None, :]) / t
    twc = jnp.asarray(np.broadcast_to(np.cos(ang)[:, :, None], (t2, t1, LANES)), F32)
    tws = jnp.asarray(np.broadcast_to(np.sin(ang)[:, :, None], (t2, t1, LANES)), F32)
    c2, s2 = _dft_mats(t2)
    f2 = jnp.asarray(np.concatenate([c2, s2], axis=1), BF16)
    z4 = z.reshape(b, t1, t2, w2)
    bmid = pl.pallas_call(
        functools.partial(_dft_stage1_body, gd=gd),
        grid=(b, FNET_GROUPS, t2 // FFT_ROWS),
        in_specs=[pl.BlockSpec((1, t1, FFT_ROWS, 2 * gd), lambda bi, g, i: (bi, 0, i, g)),
                  pl.BlockSpec((2 * t1, 2 * t1), lambda bi, g, i: (0, 0)),
                  pl.BlockSpec((FFT_ROWS, t1, LANES), lambda bi, g, i: (i, 0, 0)),
                  pl.BlockSpec((FFT_ROWS, t1, LANES), lambda bi, g, i: (i, 0, 0))],
        out_specs=pl.BlockSpec((1, 1, t1, FFT_ROWS, 2 * gd), lambda bi, g, i: (bi, g, 0, i, 0)),
        out_shape=jax.ShapeDtypeStruct((b, FNET_GROUPS, t1, t2, 2 * gd), F32),
        compiler_params=_cparams("parallel", "parallel", "parallel"),
        name="token_dft_stage1",
    )(z4, f1, twc, tws)
    y = pl.pallas_call(
        functools.partial(_dft_stage2_body, gd=gd, scale=float((t * gd) ** -0.5)),
        grid=(b, FNET_GROUPS, t1 // FFT_ROWS),
        in_specs=[pl.BlockSpec((1, 1, FFT_ROWS, t2, 2 * gd), lambda bi, g, i: (bi, g, i, 0, 0)),
                  pl.BlockSpec((t2, 2 * t2), lambda bi, g, i: (0, 0))],
        out_specs=pl.BlockSpec((1, t2, FFT_ROWS, gd), lambda bi, g, i: (bi, 0, i, g)),
        out_shape=jax.ShapeDtypeStruct((b, t2, t1, w2 // 2), F32),
        compiler_params=_cparams("parallel", "parallel", "parallel"),
        name="token_dft_stage2",
    )(bmid, f2)
    return y.reshape(b, t, w2 // 2)


def fourier_mix(p, *, u_off, fd):
    gd = fd // FNET_GROUPS
    z = channel_dft(p, u_off=u_off, fd=fd)
    if z.shape[1] <= 512:
        return token_dft_direct(z, gd)
    return token_dft_two_stage(z, gd)


def _mm2_res_body(a1_ref, a2_ref, w1_ref, w2_ref, r_ref, g_ref, o_ref):
    acc = jnp.dot(a1_ref[0].astype(BF16), w1_ref[...], preferred_element_type=F32)
    acc = acc + jnp.dot(a2_ref[0].astype(BF16), w2_ref[...], preferred_element_type=F32)
    o_ref[0] = r_ref[0] + g_ref[0] * acc


def matmul2_gated_residual(a1, a2, w, res, gate, *, tm=512, tn=2048):
    b, t, k1 = a1.shape
    k2 = a2.shape[2]
    n = w.shape[1]
    tm = _row_tile(t, tm)
    tn = _row_tile(n, tn)
    return pl.pallas_call(
        _mm2_res_body,
        grid=(b, t // tm, n // tn),
        in_specs=[
            pl.BlockSpec((1, tm, k1), lambda bi, i, j: (bi, i, 0)),
            pl.BlockSpec((1, tm, k2), lambda bi, i, j: (bi, i, 0)),
            pl.BlockSpec((k1, tn), lambda bi, i, j: (0, j)),
            pl.BlockSpec((k2, tn), lambda bi, i, j: (k1 // k2, j)),
            pl.BlockSpec((1, tm, tn), lambda bi, i, j: (bi, i, j)),
            pl.BlockSpec((1, 1, tn), lambda bi, i, j: (bi, 0, j)),
        ],
        out_specs=pl.BlockSpec((1, tm, tn), lambda bi, i, j: (bi, i, j)),
        out_shape=jax.ShapeDtypeStruct((b, t, n), F32),
        compiler_params=_cparams("parallel", "parallel", "parallel"),
        name="matmul2_gated_residual",
    )(a1, a2, w, w, res, gate)


def mixer_gla_fnet(x_ctx, x_lat, mod_ctx, mod_lat, norm_g1, w_in, w_out,
                   a_up_f, a_b_f, a_up_b, a_b_b, gla_norm_g, last):
    dv = gla_norm_g.shape[0]
    dk = a_up_f.shape[1]
    t = x_lat.shape[1]
    hk = dk // GLA_HEADS
    n_main = 2 * dk + 2 * dv
    fd = w_in.shape[1] - n_main - 2 * GLA_RANK
    w_main = jnp.concatenate([w_in[:, :n_main], w_in[:, n_main + 2 * GLA_RANK:]], axis=1).astype(BF16)
    w_lr = jnp.pad(w_in[:, n_main:n_main + 2 * GLA_RANK], ((0, 0), (0, LANES - 2 * GLA_RANK))).astype(BF16)
    cs = jnp.concatenate([jnp.full((dk,), hk ** -0.5, F32), jnp.ones((w_main.shape[1] - dk,), F32)])
    p_l, lr_l = norm_mod_matmul(x_lat, norm_g1, mod_lat[0], mod_lat[1], w_main, cs, w_aux=w_lr)
    p_c, lr_c = norm_mod_matmul(x_ctx, norm_g1, mod_ctx[0], mod_ctx[1], w_main, cs, w_aux=w_lr)
    ang_row, ang_col = _axial_rope_angles(t, hk)
    cos = jnp.concatenate([jnp.cos(ang_row)] * 2 + [jnp.cos(ang_col)] * 2, axis=1)
    sin = jnp.concatenate([-jnp.sin(ang_row), jnp.sin(ang_row), -jnp.sin(ang_col), jnp.sin(ang_col)], axis=1)
    pad_f = ((0, LANES - GLA_RANK), (0, 0))
    pad_b = ((GLA_RANK, LANES - 2 * GLA_RANK), (0, 0))
    ofc, ofl = gla_scan(p_c, lr_c, p_l, lr_l, jnp.pad(a_up_f, pad_f), a_b_f[None], cos, sin,
                        dk=dk, dv=dv, reverse=False)
    obc, obl = gla_scan(p_c, lr_c, p_l, lr_l, jnp.pad(a_up_b, pad_b), a_b_b[None], cos, sin,
                        dk=dk, dv=dv, reverse=True)
    w_out_b = w_out.astype(BF16)
    g_off, u_off = 2 * dk + dv, 2 * dk + 2 * dv
    x_lat = matmul2_gated_residual(gla_output(ofl, obl, p_l, gla_norm_g, g_off=g_off),
                                   fourier_mix(p_l, u_off=u_off, fd=fd), w_out_b, x_lat, mod_lat[2])
    if not last:
        x_ctx = matmul2_gated_residual(gla_output(ofc, obc, p_c, gla_norm_g, g_off=g_off),
                                       fourier_mix(p_c, u_off=u_off, fd=fd), w_out_b, x_ctx, mod_ctx[2])
    return x_ctx, x_lat


def _ctx_attn_body(q_ref, k_ref, v_ref, o_ref, *, heads, hd):
    nt = (((1,), (1,)), ((), ()))
    for h in range(heads):
        sl = slice(h * hd, (h + 1) * hd)
        s = lax.dot_general(q_ref[0, :, sl], k_ref[0, :, sl], nt, preferred_element_type=F32)
        p = jnp.exp(s - jnp.max(s, axis=-1, keepdims=True))
        o = jnp.dot(p.astype(BF16), v_ref[0, :, sl], preferred_element_type=F32)
        o_ref[0, :, sl] = (o / jnp.sum(p, axis=-1, keepdims=True)).astype(o_ref.dtype)


def context_self_attention(qkv_ctx):
    b, l, d3 = qkv_ctx.shape
    d = d3 // 3
    hd = d // NA_HEADS
    g = NA_HEADS_PER_STEP
    gw = g * hd
    nhg = NA_HEADS // g
    return pl.pallas_call(
        functools.partial(_ctx_attn_body, heads=g, hd=hd),
        grid=(b, nhg),
        in_specs=[pl.BlockSpec((1, l, gw), lambda bi, hg: (bi, 0, hg)),
                  pl.BlockSpec((1, l, gw), lambda bi, hg: (bi, 0, nhg + hg)),
                  pl.BlockSpec((1, l, gw), lambda bi, hg: (bi, 0, 2 * nhg + hg))],
        out_specs=pl.BlockSpec((1, l, gw), lambda bi, hg: (bi, 0, hg)),
        out_shape=jax.ShapeDtypeStruct((b, l, d), BF16),
        compiler_params=_cparams("parallel", "parallel"),
        name="context_self_attention",
    )(qkv_ctx, qkv_ctx, qkv_ctx)


def mixer_neighbourhood(x_ctx, x_lat, mod_ctx, mod_lat, norm_g1, w_qkv, w_out, rpb, last):
    bsz, t, d = x_lat.shape
    l = x_ctx.shape[1]
    hd = d // NA_HEADS
    w_qkv_b = w_qkv.astype(BF16)
    w_out_b = w_out.astype(BF16)
    cs = jnp.concatenate([jnp.full((d,), hd ** -0.5, F32), jnp.ones((2 * d,), F32)])
    qkv_l = norm_mod_matmul(x_lat, norm_g1, mod_lat[0], mod_lat[1], w_qkv_b, cs)
    qkv_c = norm_mod_matmul(x_ctx, norm_g1, mod_ctx[0], mod_ctx[1], w_qkv_b, cs)
    o_l = neighbourhood_attention(qkv_l, qkv_c, rpb)
    x_lat = matmul_gated_residual(o_l, w_out_b, x_lat, mod_lat[2])
    if not last:
        x_ctx = matmul_gated_residual(context_self_attention(qkv_c), w_out_b, x_ctx, mod_ctx[2])
    return x_ctx, x_lat


def kernel(x, c, ctx, c_ctx, ada_w, ada_b, norm1_g, norm2_g, ab_w_in, ab_w_out, gla_a_up_f, gla_a_b_f,
           gla_a_up_b, gla_a_b_b, gla_norm_g, na_w_qkv, na_w_out, na_rpb, router_w, exp_w1, exp_w3,
           exp_w2, final_g):
    depth = ada_w.shape[0]
    bsz, t, d = x.shape
    cond = jnp.concatenate([c, c_ctx[None], jnp.zeros((8 - bsz - 1, d), F32)], axis=0)
    mods = ada_all_layers(cond, ada_w, ada_b)
    w1, w3, w2 = exp_w1.astype(BF16), exp_w3.astype(BF16), exp_w2.astype(BF16)
    xl, xc = x, ctx
    for l in range(depth):
        last = l == depth - 1
        m = mods[l].reshape(8, N_MOD, d)
        mod_lat = [m[:bsz, i][:, None, :] for i in range(N_MOD)]
        mod_ctx = [jnp.broadcast_to(m[bsz, i][None, None, :], (bsz, 1, d)) for i in range(N_MOD)]
        if l % 2 == 0:
            e = l // 2
            xc, xl = mixer_gla_fnet(xc, xl, mod_ctx[:3], mod_lat[:3], norm1_g[l], ab_w_in[e], ab_w_out[e],
                                    gla_a_up_f[e], gla_a_b_f[e], gla_a_up_b[e], gla_a_b_b[e],
                                    gla_norm_g[e], last)
        else:
            o = l // 2
            xc, xl = mixer_neighbourhood(xc, xl, mod_ctx[:3], mod_lat[:3], norm1_g[l], na_w_qkv[o],
                                         na_w_out[o], na_rpb[o], last)
        xl = expert_choice_moe(xl, norm2_g[l], mod_lat[3], mod_lat[4], mod_lat[5], router_w[l], w1, w3, w2, l)
        if not last:
            xc = expert_choice_moe(xc, norm2_g[l], mod_ctx[3], mod_ctx[4], mod_ctx[5], router_w[l], w1, w3, w2, l)
    return rmsnorm_final(xl, final_g)
```
